```python
import jax, jax.numpy as jnp
from jax import lax
import numpy as np

D_MODEL = 1024
BATCH = 1
SEQ = 16384
DEPTH = 2

GRID_W = 64
CTX_LEN = 256
HEAD_DIM = 64
N_Q_HEADS = 8
N_KV_HEADS = 2
GQA_GROUP = N_Q_HEADS // N_KV_HEADS
Q_W = N_Q_HEADS * HEAD_DIM
KV_W = N_KV_HEADS * HEAD_DIM
F_GROUPS = 8
F_GROUP_DIM = 64
F_W = F_GROUPS * F_GROUP_DIM
HYB_IN = Q_W + 2 * KV_W + F_W
HYB_OUT = Q_W + F_W
ROPE_HALF = HEAD_DIM // 2
ROPE_THETA = 10000.0
Q_BLOCK = 128
CONV_DIM = D_MODEL
CONV_WIDTH = 31
FFN_DIM = 2816
FFN_CONV_WIDTH = 3
N_EVEN = (DEPTH + 1) // 2
N_ODD = DEPTH // 2
NORM_EPS = 1e-6
LN_EPS = 1e-5

kernel_name = "hybrid_fourier_gqa_conformer_dit"


def rmsnorm(x, g):
    xf = x.astype(jnp.float32)
    y = xf * lax.rsqrt(jnp.mean(xf * xf, axis=-1, keepdims=True) + NORM_EPS)
    return (y * g.astype(jnp.float32)).astype(x.dtype)


def layernorm(x, g, b):
    xf = x.astype(jnp.float32)
    mu = jnp.mean(xf, axis=-1, keepdims=True)
    xc = xf - mu
    var = jnp.mean(xc * xc, axis=-1, keepdims=True)
    y = xc * lax.rsqrt(var + LN_EPS) * g.astype(jnp.float32) + b.astype(jnp.float32)
    return y.astype(x.dtype)


def modulate(h, shift, scale):
    return h * (1 + scale) + shift


def ada_params(cond, w_ada_i, b_ada_i):
    m = jax.nn.silu(cond) @ w_ada_i + b_ada_i
    m = m.reshape(m.shape[:-1] + (1, m.shape[-1]))
    if m.ndim == 2:
        m = m[None]
    return jnp.split(m, 6, axis=-1)


def dwconv(x, w, b):
    k = w.shape[0]
    pad = (k - 1) // 2
    y = lax.conv_general_dilated(x, w[:, None, :].astype(x.dtype), window_strides=(1,),
                                 padding=[(pad, pad)], dimension_numbers=("NWC", "WIO", "NWC"),
                                 feature_group_count=x.shape[-1])
    return y + b


def axial_rope(x):
    L = x.shape[1]
    rows = L // GRID_W
    row = jnp.repeat(jnp.arange(rows, dtype=jnp.float32), GRID_W)
    col = jnp.tile(jnp.arange(GRID_W, dtype=jnp.float32), rows)
    inv_freq = 1.0 / (ROPE_THETA ** (jnp.arange(0, ROPE_HALF, 2, dtype=jnp.float32) / ROPE_HALF))
    bshape = (1, L) + (1,) * (x.ndim - 3) + (ROPE_HALF,)

    def rot(xh, pos):
        ang = pos[:, None] * inv_freq[None, :]
        ang = jnp.concatenate([ang, ang], axis=-1).reshape(bshape)
        x1, x2 = jnp.split(xh, 2, axis=-1)
        return xh * jnp.cos(ang) + jnp.concatenate([-x2, x1], axis=-1) * jnp.sin(ang)

    xf = x.astype(jnp.float32)
    out = jnp.concatenate([rot(xf[..., :ROPE_HALF], row), rot(xf[..., ROPE_HALF:], col)], axis=-1)
    return out.astype(x.dtype)


def attend(qb, k, v):
    s = jnp.einsum("bqkgd,bskd->bkgqs", qb, k, preferred_element_type=jnp.float32) * (HEAD_DIM ** -0.5)
    p = jax.nn.softmax(s, axis=-1)
    return jnp.einsum("bkgqs,bskd->bqkgd", p.astype(v.dtype), v)


def blocked_attention(q, k, v):
    B, L = q.shape[0], q.shape[1]
    nb = L // Q_BLOCK
    qb = q.reshape(B, nb, Q_BLOCK, N_KV_HEADS, GQA_GROUP, HEAD_DIM).swapaxes(0, 1)
    o = lax.map(lambda t: attend(t, k, v), qb)
    return o.swapaxes(0, 1).reshape(B, L, Q_W)


def fourier_mix(f):
    B, L = f.shape[0], f.shape[1]
    g = f.reshape(B, L, F_GROUPS, F_GROUP_DIM).astype(jnp.float32)
    y = jnp.fft.fftn(g, axes=(1, 3), norm="ortho").real
    return y.reshape(B, L, F_W).astype(f.dtype)


def split_heads(u):
    B, L = u.shape[0], u.shape[1]
    q = u[..., :Q_W].reshape(B, L, N_KV_HEADS, GQA_GROUP, HEAD_DIM)
    k = u[..., Q_W:Q_W + KV_W].reshape(B, L, N_KV_HEADS, HEAD_DIM)
    v = u[..., Q_W + KV_W:Q_W + 2 * KV_W].reshape(B, L, N_KV_HEADS, HEAD_DIM)
    f = u[..., Q_W + 2 * KV_W:]
    return q, k, v, f


def hybrid_mixer(h_lat, h_ctx, w_in, q_gain, k_gain, w_out, need_ctx_out):
    B, L = h_lat.shape[0], h_lat.shape[1]
    q_l, k_l, v_l, f_l = split_heads(h_lat @ w_in)
    q_l = axial_rope(rmsnorm(q_l, q_gain))
    k_l = axial_rope(rmsnorm(k_l, k_gain))
    if need_ctx_out:
        q_c, k_c, v_c, f_c = split_heads(h_ctx @ w_in)
        q_c = rmsnorm(q_c, q_gain)
    else:
        kv_c = h_ctx @ w_in[:, Q_W:Q_W + 2 * KV_W]
        Lc = h_ctx.shape[1]
        k_c = kv_c[..., :KV_W].reshape(B, Lc, N_KV_HEADS, HEAD_DIM)
        v_c = kv_c[..., KV_W:].reshape(B, Lc, N_KV_HEADS, HEAD_DIM)
    k_c = rmsnorm(k_c, k_gain)
    k_all = jnp.concatenate([k_l, k_c], axis=1)
    v_all = jnp.concatenate([v_l, v_c], axis=1)
    a_l = blocked_attention(q_l, k_all, v_all)
    o_l = jnp.concatenate([a_l, fourier_mix(f_l)], axis=-1) @ w_out
    if need_ctx_out:
        a_c = attend(q_c, k_c, v_c).reshape(B, h_ctx.shape[1], Q_W)
        o_c = jnp.concatenate([a_c, fourier_mix(f_c)], axis=-1) @ w_out
        return o_l, o_c
    return o_l, None


def conformer_conv(h, w_pw1, b_pw1, w_dw, b_dw, ln_g, ln_b, w_pw2, b_pw2):
    u = h @ w_pw1 + b_pw1
    a, g = jnp.split(u, 2, axis=-1)
    u = a * jax.nn.sigmoid(g)
    u = dwconv(u, w_dw, b_dw)
    u = jax.nn.silu(layernorm(u, ln_g, ln_b))
    return u @ w_pw2 + b_pw2


def conv_ffn(h, w_up, w_dw, b_dw, w_down):
    u = dwconv(h @ w_up, w_dw, b_dw)
    a, b = jnp.split(u, 2, axis=-1)
    return (jax.nn.silu(a) * b) @ w_down


def setup_inputs(seed: int = 0) -> dict:
    key = jax.random.key(seed)
    ks = jax.random.split(key, 32)
    D = D_MODEL
    nrm = lambda k, shape, s: jax.random.normal(k, shape, jnp.float32) * s
    return {
        "x": nrm(ks[0], (BATCH, SEQ, D), 1.0),
        "c": nrm(ks[1], (BATCH, D), 1.0),
        "ctx": nrm(ks[2], (BATCH, CTX_LEN, D), 1.0),
        "c_ctx": nrm(ks[3], (D,), 1.0),
        "w_ada": nrm(ks[4], (DEPTH, D, 6 * D), D ** -0.5),
        "b_ada": nrm(ks[5], (DEPTH, 6 * D), 0.02),
        "g_mix": 1.0 + nrm(ks[6], (DEPTH, D), 0.02),
        "g_ffn": 1.0 + nrm(ks[7], (DEPTH, D), 0.02),
        "w_in_hyb": nrm(ks[8], (N_EVEN, D, HYB_IN), D ** -0.5),
        "q_gain": 1.0 + nrm(ks[9], (N_EVEN, HEAD_DIM), 0.02),
        "k_gain": 1.0 + nrm(ks[10], (N_EVEN, HEAD_DIM), 0.02),
        "w_out_hyb": nrm(ks[11], (N_EVEN, HYB_OUT, D), HYB_OUT ** -0.5),
        "w_pw1": nrm(ks[12], (N_ODD, D, 2 * CONV_DIM), D ** -0.5),
        "b_pw1": nrm(ks[13], (N_ODD, 2 * CONV_DIM), 0.02),
        "w_cdw": nrm(ks[14], (N_ODD, CONV_WIDTH, CONV_DIM), CONV_WIDTH ** -0.5),
        "b_cdw": nrm(ks[15], (N_ODD, CONV_DIM), 0.02),
        "ln_g": 1.0 + nrm(ks[16], (N_ODD, CONV_DIM), 0.02),
        "ln_b": nrm(ks[17], (N_ODD, CONV_DIM), 0.02),
        "w_pw2": nrm(ks[18], (N_ODD, CONV_DIM, D), CONV_DIM ** -0.5),
        "b_pw2": nrm(ks[19], (N_ODD, D), 0.02),
        "w_up": nrm(ks[20], (DEPTH, D, 2 * FFN_DIM), D ** -0.5),
        "w_fdw": nrm(ks[21], (DEPTH, FFN_CONV_WIDTH, 2 * FFN_DIM), FFN_CONV_WIDTH ** -0.5),
        "b_fdw": nrm(ks[22], (DEPTH, 2 * FFN_DIM), 0.02),
        "w_down": nrm(ks[23], (DEPTH, FFN_DIM, D), FFN_DIM ** -0.5),
    }


def reference(x, c, ctx, c_ctx, w_ada, b_ada, g_mix, g_ffn, w_in_hyb, q_gain, k_gain, w_out_hyb,
              w_pw1, b_pw1, w_cdw, b_cdw, ln_g, ln_b, w_pw2, b_pw2, w_up, w_fdw, b_fdw, w_down):
    x_lat = x
    x_ctx = ctx
    for i in range(DEPTH):
        j = i // 2
        is_even = (i % 2 == 0)
        ctx_next = any(l % 2 == 0 for l in range(i + 1, DEPTH))
        ctx_here = is_even or ctx_next
        sh1, sc1, g1, sh2, sc2, g2 = ada_params(c, w_ada[i], b_ada[i])
        csh1, csc1, cg1, csh2, csc2, cg2 = ada_params(c_ctx, w_ada[i], b_ada[i])

        h_lat = modulate(rmsnorm(x_lat, g_mix[i]), sh1, sc1)
        h_ctx = modulate(rmsnorm(x_ctx, g_mix[i]), csh1, csc1) if ctx_here else None
        if is_even:
            o_lat, o_ctx = hybrid_mixer(h_lat, h_ctx, w_in_hyb[j], q_gain[j], k_gain[j], w_out_hyb[j], ctx_next)
        else:
            o_lat = conformer_conv(h_lat, w_pw1[j], b_pw1[j], w_cdw[j], b_cdw[j], ln_g[j], ln_b[j], w_pw2[j], b_pw2[j])
            o_ctx = conformer_conv(h_ctx, w_pw1[j], b_pw1[j], w_cdw[j], b_cdw[j], ln_g[j], ln_b[j], w_pw2[j], b_pw2[j]) if ctx_next else None
        x_lat = x_lat + g1 * o_lat
        h = modulate(rmsnorm(x_lat, g_ffn[i]), sh2, sc2)
        x_lat = x_lat + g2 * conv_ffn(h, w_up[i], w_fdw[i], b_fdw[i], w_down[i])
        if ctx_next:
            x_ctx = x_ctx + cg1 * o_ctx
            hc = modulate(rmsnorm(x_ctx, g_ffn[i]), csh2, csc2)
            x_ctx = x_ctx + cg2 * conv_ffn(hc, w_up[i], w_fdw[i], b_fdw[i], w_down[i])
    return x_lat
```

```python
import functools
import math

import numpy as np
import jax
import jax.numpy as jnp
from jax import lax
from jax.experimental import pallas as pl
from jax.experimental.pallas import tpu as pltpu

F32 = jnp.float32
BF16 = jnp.bfloat16

D_MODEL = 1024
GRID_W = 64
HEAD_DIM = 64
N_Q_HEADS = 8
N_KV_HEADS = 2
GQA_GROUP = N_Q_HEADS // N_KV_HEADS
Q_W = N_Q_HEADS * HEAD_DIM
KV_W = N_KV_HEADS * HEAD_DIM
F_GROUPS = 8
F_GROUP_DIM = 64
F_W = F_GROUPS * F_GROUP_DIM
HYB_IN = Q_W + 2 * KV_W + F_W
ROPE_HALF = HEAD_DIM // 2
ROPE_THETA = 10000.0
CONV_WIDTH = 31
FFN_DIM = 2816
NORM_EPS = 1e-6
LN_EPS = 1e-5

LANES = 128
SUBLANES = 8
FFT_N = 128
V_ROWS = 80
VMEM_LIMIT = 56 * 1024 * 1024

Q_SCALE = HEAD_DIM ** -0.5 * math.log2(math.e)


def _cparams(n_axes=1):
    return pltpu.CompilerParams(dimension_semantics=("arbitrary",) * n_axes,
                                vmem_limit_bytes=VMEM_LIMIT)


def _const_spec(shape):
    zeros = (0,) * len(shape)
    return pl.BlockSpec(shape, lambda *_: zeros, pipeline_mode=pl.Buffered(1))


def _rms_mod(x, g, shift, scale):
    ms = jnp.mean(x * x, axis=-1, keepdims=True)
    return (x * lax.rsqrt(ms + NORM_EPS) * g) * (1.0 + scale) + shift


def _silu(x):
    return x * (1.0 / (1.0 + jnp.exp(-x)))


def _ada_kernel(cond_ref, w_ref, b_ref, o_ref):
    cnd = cond_ref[...]
    o_ref[0] = jnp.dot(_silu(cnd), w_ref[0], preferred_element_type=F32,
                       precision=lax.Precision.HIGHEST) + b_ref[0]


def _ada(cond, w_ada, b_ada):
    depth, d, n = w_ada.shape
    tn = 1536
    return pl.pallas_call(
        _ada_kernel,
        grid=(depth, n // tn),
        in_specs=[pl.BlockSpec((SUBLANES, d), lambda i, j: (0, 0)),
                  pl.BlockSpec((1, d, tn), lambda i, j: (i, 0, j)),
                  pl.BlockSpec((1, 1, tn), lambda i, j: (i, 0, j))],
        out_specs=pl.BlockSpec((1, SUBLANES, tn), lambda i, j: (i, 0, j)),
        out_shape=jax.ShapeDtypeStruct((depth, SUBLANES, n), F32),
        compiler_params=_cparams(2),
        name="ada",
    )(cond, w_ada, b_ada.reshape(depth, 1, n))


def _inproj_kernel(x_ref, gm_ref, sh_ref, sc_ref, w_ref, qg_ref, kg_ref, cos_ref, sin_ref,
                   bd_ref, dft_ref, q_ref, k_ref, v_ref, ab_ref):
    h = _rms_mod(x_ref[...], gm_ref[...], sh_ref[...], sc_ref[...])
    u = jnp.dot(h.astype(BF16), w_ref[...], preferred_element_type=F32)
    cos = cos_ref[...]
    sin = sin_ref[...]
    bd = bd_ref[...]
    lane = lax.broadcasted_iota(jnp.int32, cos.shape, 1)
    first = (lane % ROPE_HALF) < (ROPE_HALF // 2)

    def head_norm_rope(t, gain):
        sq = t * t
        hi = sq.astype(BF16)
        lo = (sq - hi.astype(F32)).astype(BF16)
        ms = (jnp.dot(hi, bd, preferred_element_type=F32)
              + jnp.dot(lo, bd, preferred_element_type=F32))
        tn = t * lax.rsqrt(ms + NORM_EPS) * gain
        partner = jnp.where(first, pltpu.roll(tn, LANES - ROPE_HALF // 2, 1),
                            pltpu.roll(tn, ROPE_HALF // 2, 1))
        return tn * cos + partner * sin

    qg = qg_ref[...]
    for j in range(Q_W // LANES):
        t = head_norm_rope(u[:, j * LANES:(j + 1) * LANES], qg)
        q_ref[:, j * LANES:(j + 1) * LANES] = (t * Q_SCALE).astype(BF16)
    k_ref[...] = head_norm_rope(u[:, Q_W:Q_W + KV_W], kg_ref[...]).astype(BF16)
    v_ref[...] = u[:, Q_W + KV_W:Q_W + 2 * KV_W].astype(BF16)
    f = u[:, Q_W + 2 * KV_W:].astype(BF16)
    ab_ref[...] = jnp.dot(f, dft_ref[...], preferred_element_type=F32).astype(BF16)


def _inproj(x2d, gm, shift, scale, w_in, qg, kg, cos_t, sin_t, bd, dft):
    n = x2d.shape[0]
    tm = min(512, n)
    row = lambda w: pl.BlockSpec((tm, w), lambda i: (i, 0))
    vec = lambda w: _const_spec((1, w))
    return pl.pallas_call(
        _inproj_kernel,
        grid=(n // tm,),
        in_specs=[row(D_MODEL), vec(D_MODEL), vec(D_MODEL), vec(D_MODEL),
                  _const_spec((D_MODEL, HYB_IN)), vec(LANES), vec(LANES),
                  row(LANES), row(LANES), _const_spec((LANES, LANES)),
                  _const_spec((F_W, 2 * F_W))],
        out_specs=[row(Q_W), row(KV_W), row(KV_W), row(2 * F_W)],
        out_shape=[jax.ShapeDtypeStruct((n, Q_W), BF16), jax.ShapeDtypeStruct((n, KV_W), BF16),
                   jax.ShapeDtypeStruct((n, KV_W), BF16), jax.ShapeDtypeStruct((n, 2 * F_W), BF16)],
        compiler_params=_cparams(1),
        name="inproj",
    )(x2d, gm, shift, scale, w_in, qg, kg, cos_t, sin_t, bd, dft)


def _rope_tables(n):
    inv_freq = 1.0 / (ROPE_THETA ** (np.arange(0, ROPE_HALF, 2, dtype=np.float64) / ROPE_HALF))
    rows = n // GRID_W
    e = np.arange(ROPE_HALF)
    sign = np.where(e < ROPE_HALF // 2, -1.0, 1.0)
    ang_r = np.arange(rows)[:, None] * inv_freq[e % (ROPE_HALF // 2)][None, :]
    ang_c = np.arange(GRID_W)[:, None] * inv_freq[e % (ROPE_HALF // 2)][None, :]

    def build(fr, fc):
        tr = jnp.broadcast_to(jnp.asarray(fr, F32)[:, None, :], (rows, GRID_W, ROPE_HALF))
        tc = jnp.broadcast_to(jnp.asarray(fc, F32)[None, :, :], (rows, GRID_W, ROPE_HALF))
        t = jnp.concatenate([tr, tc], axis=-1).reshape(n, HEAD_DIM)
        return jnp.concatenate([t, t], axis=-1)

    return build(np.cos(ang_r), np.cos(ang_c)), build(np.sin(ang_r) * sign, np.sin(ang_c) * sign)


def _headnorm_matrix():
    i = np.arange(LANES)
    return jnp.asarray((i[:, None] // HEAD_DIM == i[None, :] // HEAD_DIM) / HEAD_DIM, BF16)


def _channel_dft_matrix():
    i = np.arange(F_W)
    same = (i[:, None] // F_GROUP_DIM == i[None, :] // F_GROUP_DIM)
    ang = 2.0 * np.pi * ((i[:, None] % F_GROUP_DIM) * (i[None, :] % F_GROUP_DIM) % F_GROUP_DIM) / F_GROUP_DIM
    s = F_GROUP_DIM ** -0.5
    return jnp.asarray(np.concatenate([np.cos(ang) * same * s, np.sin(ang) * same * s], axis=1), BF16)


def _attn_kernel(qt_ref, k_ref, vt_ref, o_ref, *, tk):
    lk = k_ref.shape[0]
    tq = qt_ref.shape[1]
    outs = []
    for g in range(GQA_GROUP):
        qt = qt_ref[g * HEAD_DIM:(g + 1) * HEAD_DIM, :]

        def body(kb, acc):
            off = pl.multiple_of(kb * tk, tk)
            st = jnp.dot(k_ref[pl.ds(off, tk), :], qt, preferred_element_type=F32)
            pt = jnp.exp2(st).astype(BF16)
            return acc + jnp.dot(vt_ref[:, pl.ds(off, tk)], pt, preferred_element_type=F32)

        acc = lax.fori_loop(0, lk // tk, body, jnp.zeros((V_ROWS, tq), F32))
        o = acc[:HEAD_DIM, :] * (1.0 / acc[HEAD_DIM:HEAD_DIM + 1, :])
        outs.append(o.T)
    o_ref[...] = jnp.concatenate(outs, axis=1).astype(o_ref.dtype)


def _attention(qt, k, vt, tq=256, tk=1280):
    l = qt.shape[1]
    lk = k.shape[1]
    gw = GQA_GROUP * HEAD_DIM
    return pl.pallas_call(
        functools.partial(_attn_kernel, tk=tk),
        grid=(N_KV_HEADS, l // tq),
        in_specs=[pl.BlockSpec((gw, tq), lambda h, i: (h, i)),
                  pl.BlockSpec((None, lk, HEAD_DIM), lambda h, i: (h, 0, 0)),
                  pl.BlockSpec((None, V_ROWS, lk), lambda h, i: (h, 0, 0))],
        out_specs=pl.BlockSpec((tq, gw), lambda h, i: (i, h)),
        out_shape=jax.ShapeDtypeStruct((l, Q_W), BF16),
        compiler_params=_cparams(2),
        name="attention",
    )(qt, k, vt)


def _fft1_kernel(ab_ref, ca_ref, cb_ref, tr_ref, ti_ref, *, nblk):
    ca = ca_ref[...]
    cb = cb_ref[...]
    for j in range(nblk):
        a = ab_ref[:, j * 2 * F_W:j * 2 * F_W + F_W]
        b = ab_ref[:, j * 2 * F_W + F_W:(j + 1) * 2 * F_W]
        t = (jnp.dot(ca, a, preferred_element_type=F32)
             + jnp.dot(cb, b, preferred_element_type=F32))
        tr_ref[:, j * F_W:(j + 1) * F_W] = t[:FFT_N].astype(BF16)
        ti_ref[:, j * F_W:(j + 1) * F_W] = t[FFT_N:].astype(BF16)


def _fft2_kernel(tr_ref, ti_ref, m_ref, y_ref, *, nblk):
    for j in range(nblk):
        m = m_ref[j]
        y = (jnp.dot(m[:, :FFT_N], tr_ref[j * FFT_N:(j + 1) * FFT_N, :], preferred_element_type=F32)
             + jnp.dot(m[:, FFT_N:], ti_ref[j * FFT_N:(j + 1) * FFT_N, :], preferred_element_type=F32))
        y_ref[:, j * F_W:(j + 1) * F_W] = y.astype(BF16)


def _fft_tables():
    n = FFT_N
    l = n * n
    s = n ** -0.5
    k = np.arange(n)
    ang1 = 2.0 * np.pi * ((k[:, None] * k[None, :]) % n) / n
    c1, s1 = np.cos(ang1) * s, np.sin(ang1) * s
    ca = np.concatenate([c1, -s1], axis=0)
    cb = np.concatenate([-s1, -c1], axis=0)
    k1 = k[:, None, None]
    k2 = k[None, :, None]
    n2 = k[None, None, :]
    ang2 = 2.0 * np.pi * ((n2 * (k1 + n * k2)) % l) / l
    m = np.concatenate([np.cos(ang2) * s, np.sin(ang2) * s], axis=2)
    return jnp.asarray(ca, BF16), jnp.asarray(cb, BF16), jnp.asarray(m, BF16)


def _fourier(ab):
    l = ab.shape[0]
    assert l == FFT_N * FFT_N
    ca, cb, m = _fft_tables()
    nblk = 16
    ab_v = ab.reshape(FFT_N, FFT_N * 2 * F_W)
    tr, ti = pl.pallas_call(
        functools.partial(_fft1_kernel, nblk=nblk),
        grid=(FFT_N // nblk,),
        in_specs=[pl.BlockSpec((FFT_N, nblk * 2 * F_W), lambda i: (0, i)),
                  _const_spec((2 * FFT_N, FFT_N)), _const_spec((2 * FFT_N, FFT_N))],
        out_specs=[pl.BlockSpec((FFT_N, nblk * F_W), lambda i: (0, i))] * 2,
        out_shape=[jax.ShapeDtypeStruct((FFT_N, FFT_N * F_W), BF16)] * 2,
        compiler_params=_cparams(1),
        name="fft_stage1",
    )(ab_v, ca, cb)
    tr = tr.reshape(l, F_W)
    ti = ti.reshape(l, F_W)
    y = pl.pallas_call(
        functools.partial(_fft2_kernel, nblk=nblk),
        grid=(FFT_N // nblk,),
        in_specs=[pl.BlockSpec((nblk * FFT_N, F_W), lambda i: (i, 0)),
                  pl.BlockSpec((nblk * FFT_N, F_W), lambda i: (i, 0)),
                  pl.BlockSpec((nblk, FFT_N, 2 * FFT_N), lambda i: (i, 0, 0))],
        out_specs=pl.BlockSpec((FFT_N, nblk * F_W), lambda i: (0, i)),
        out_shape=jax.ShapeDtypeStruct((FFT_N, FFT_N * F_W), BF16),
        compiler_params=_cparams(1),
        name="fft_stage2",
    )(tr, ti, m)
    return y.reshape(l, F_W)


def _outproj_kernel(x_ref, a_ref, y_ref, w_ref, gate_ref, o_ref):
    o = (jnp.dot(a_ref[...], w_ref[:Q_W, :], preferred_element_type=F32)
         + jnp.dot(y_ref[...], w_ref[Q_W:, :], preferred_element_type=F32))
    o_ref[...] = x_ref[...] + gate_ref[...] * o


def _outproj(x2d, a, y, w_out, gate):
    n = x2d.shape[0]
    tm = 512
    row = lambda w: pl.BlockSpec((tm, w), lambda i: (i, 0))
    return pl.pallas_call(
        _outproj_kernel,
        grid=(n // tm,),
        in_specs=[row(D_MODEL), row(Q_W), row(F_W), _const_spec((Q_W + F_W, D_MODEL)),
                  _const_spec((1, D_MODEL))],
        out_specs=row(D_MODEL),
        out_shape=jax.ShapeDtypeStruct((n, D_MODEL), F32),
        compiler_params=_cparams(1),
        name="outproj",
    )(x2d, a, y, w_out, gate)


FFN_HALO = SUBLANES
FFN_CHUNKS = 2


def _ffn_kernel(xp_ref, x_ref, xn_ref, g_ref, sh_ref, sc_ref, gate_ref, wup_ref, wdw_ref, bdw_ref,
                wdn_ref, o_ref):
    i = pl.program_id(0)
    last = pl.num_programs(0) - 1
    tm = x_ref.shape[0]
    g, sh, sc = g_ref[...], sh_ref[...], sc_ref[...]
    x = x_ref[...]
    hp = _rms_mod(xp_ref[...], g, sh, sc) * jnp.where(i > 0, 1.0, 0.0)
    hn = _rms_mod(xn_ref[...], g, sh, sc) * jnp.where(i < last, 1.0, 0.0)
    h = jnp.concatenate([hp, _rms_mod(x, g, sh, sc), hn], axis=0).astype(BF16)
    rows = tm + 2 * FFN_HALO
    cw = FFN_DIM // FFN_CHUNKS

    def conv(u, c0):
        w = wdw_ref[:, c0:c0 + cw]
        um = pltpu.roll(u, 1, 0)[FFN_HALO:FFN_HALO + tm]
        up = pltpu.roll(u, rows - 1, 0)[FFN_HALO:FFN_HALO + tm]
        return (um * w[0:1] + u[FFN_HALO:FFN_HALO + tm] * w[1:2] + up * w[2:3]
                + bdw_ref[:, c0:c0 + cw])

    acc = None
    for c in range(FFN_CHUNKS):
        ca, cb = c * cw, FFN_DIM + c * cw
        a = conv(jnp.dot(h, wup_ref[:, ca:ca + cw], preferred_element_type=F32), ca)
        b = conv(jnp.dot(h, wup_ref[:, cb:cb + cw], preferred_element_type=F32), cb)
        act = (_silu(a) * b).astype(BF16)
        d = jnp.dot(act, wdn_ref[ca:ca + cw, :], preferred_element_type=F32)
        acc = d if acc is None else acc + d
    o_ref[...] = x + gate_ref[...] * acc


def _ffn(x2d, g, shift, scale, gate, w_up, w_dw, b_dw, w_down):
    n = x2d.shape[0]
    tm = 512
    hb = tm // FFN_HALO
    nh = n // FFN_HALO
    vec = lambda: _const_spec((1, D_MODEL))
    return pl.pallas_call(
        _ffn_kernel,
        grid=(n // tm,),
        in_specs=[pl.BlockSpec((FFN_HALO, D_MODEL), lambda i: (jnp.maximum(i * hb - 1, 0), 0)),
                  pl.BlockSpec((tm, D_MODEL), lambda i: (i, 0)),
                  pl.BlockSpec((FFN_HALO, D_MODEL), lambda i: (jnp.minimum((i + 1) * hb, nh - 1), 0)),
                  vec(), vec(), vec(), vec(),
                  _const_spec((D_MODEL, 2 * FFN_DIM)), _const_spec((3, 2 * FFN_DIM)),
                  _const_spec((1, 2 * FFN_DIM)), _const_spec((FFN_DIM, D_MODEL))],
        out_specs=pl.BlockSpec((tm, D_MODEL), lambda i: (i, 0)),
        out_shape=jax.ShapeDtypeStruct((n, D_MODEL), F32),
        compiler_params=_cparams(1),
        name="ffn",
    )(x2d, x2d, x2d, g, shift, scale, gate, w_up, w_dw, b_dw, w_down)


CONF_HALO = 2 * SUBLANES


def _conf_kernel(xp_ref, x_ref, xn_ref, g_ref, sh_ref, sc_ref, gate_ref, w1_ref, b1_ref, wdw_ref,
                 bdw_ref, lng_ref, lnb_ref, w2_ref, b2_ref, o_ref, u_scr, *, seq_len):
    i = pl.program_id(0)
    tm = x_ref.shape[0]
    g, sh, sc = g_ref[...], sh_ref[...], sc_ref[...]
    x = x_ref[...]
    h = jnp.concatenate([_rms_mod(xp_ref[...], g, sh, sc), _rms_mod(x, g, sh, sc),
                         _rms_mod(xn_ref[...], g, sh, sc)], axis=0).astype(BF16)
    u = jnp.dot(h, w1_ref[...], preferred_element_type=F32) + b1_ref[...]
    glu = u[:, :D_MODEL] * (1.0 / (1.0 + jnp.exp(-u[:, D_MODEL:])))
    pos = i * tm - CONF_HALO + lax.broadcasted_iota(jnp.int32, (tm + 2 * CONF_HALO, 1), 0)
    u_scr[...] = jnp.where((pos >= 0) & (pos < seq_len), glu, 0.0)
    pad = (CONV_WIDTH - 1) // 2
    acc = jnp.zeros((tm, D_MODEL), F32) + bdw_ref[...]
    for k in range(CONV_WIDTH):
        acc = acc + u_scr[pl.ds(CONF_HALO - pad + k, tm), :] * wdw_ref[k:k + 1, :]
    mu = jnp.mean(acc, axis=-1, keepdims=True)
    xc = acc - mu
    var = jnp.mean(xc * xc, axis=-1, keepdims=True)
    y = xc * lax.rsqrt(var + LN_EPS) * lng_ref[...] + lnb_ref[...]
    o = jnp.dot(_silu(y).astype(BF16), w2_ref[...], preferred_element_type=F32) + b2_ref[...]
    o_ref[...] = x + gate_ref[...] * o


def _conformer(x2d, g, shift, scale, gate, w1, b1, wdw, bdw, lng, lnb, w2, b2):
    n = x2d.shape[0]
    tm = 512
    hb = tm // CONF_HALO
    nh = n // CONF_HALO
    vec = lambda w=D_MODEL: _const_spec((1, w))
    return pl.pallas_call(
        functools.partial(_conf_kernel, seq_len=n),
        grid=(n // tm,),
        in_specs=[pl.BlockSpec((CONF_HALO, D_MODEL), lambda i: (jnp.maximum(i * hb - 1, 0), 0)),
                  pl.BlockSpec((tm, D_MODEL), lambda i: (i, 0)),
                  pl.BlockSpec((CONF_HALO, D_MODEL), lambda i: (jnp.minimum((i + 1) * hb, nh - 1), 0)),
                  vec(), vec(), vec(), vec(),
                  _const_spec((D_MODEL, 2 * D_MODEL)), vec(2 * D_MODEL),
                  _const_spec((CONV_WIDTH, D_MODEL)), vec(), vec(), vec(),
                  _const_spec((D_MODEL, D_MODEL)), vec()],
        out_specs=pl.BlockSpec((tm, D_MODEL), lambda i: (i, 0)),
        out_shape=jax.ShapeDtypeStruct((n, D_MODEL), F32),
        scratch_shapes=[pltpu.VMEM((tm + 2 * CONF_HALO, D_MODEL), F32)],
        compiler_params=_cparams(1),
        name="conformer",
    )(x2d, x2d, x2d, g, shift, scale, gate, w1, b1, wdw, bdw, lng, lnb, w2, b2)


def kernel(x, c, ctx, c_ctx, w_ada, b_ada, g_mix, g_ffn, w_in_hyb, q_gain, k_gain, w_out_hyb,
           w_pw1, b_pw1, w_cdw, b_cdw, ln_g, ln_b, w_pw2, b_pw2, w_up, w_fdw, b_fdw, w_down):
    batch, seq, d = x.shape
    assert batch == 1 and d == D_MODEL
    x2d = x.reshape(seq, d)
    ctx2d = ctx.reshape(-1, d)
    n_ctx = ctx2d.shape[0]
    row = lambda v: v.reshape(1, -1)

    cond = jnp.zeros((SUBLANES, d), F32).at[0].set(c[0]).at[1].set(c_ctx)
    mods = _ada(cond, w_ada, b_ada)
    mod = lambda layer, who, j: mods[layer, who:who + 1, j * d:(j + 1) * d]

    w_in = w_in_hyb[0].astype(BF16)
    qg = row(jnp.tile(q_gain[0], LANES // HEAD_DIM))
    kg = row(jnp.tile(k_gain[0], LANES // HEAD_DIM))
    bd = _headnorm_matrix()
    dft = _channel_dft_matrix()
    cos_t, sin_t = _rope_tables(seq)
    gm = row(g_mix[0])
    q, k_l, v_l, ab = _inproj(x2d, gm, mod(0, 0, 0), mod(0, 0, 1), w_in, qg, kg, cos_t, sin_t, bd, dft)
    ones_c = jnp.ones((n_ctx, LANES), F32)
    _, k_c, v_c, _ = _inproj(ctx2d, gm, mod(0, 1, 0), mod(0, 1, 1), w_in, qg, kg, ones_c,
                             jnp.zeros_like(ones_c), bd, dft)
    k_all = jnp.concatenate([k_l, k_c], axis=0)
    v_all = jnp.concatenate([v_l, v_c], axis=0)
    lk = k_all.shape[0]
    k_h = k_all.reshape(lk, N_KV_HEADS, HEAD_DIM).transpose(1, 0, 2)
    vt = v_all.reshape(lk, N_KV_HEADS, HEAD_DIM).transpose(1, 2, 0)
    vt = jnp.concatenate([vt, jnp.ones((N_KV_HEADS, 1, lk), BF16),
                          jnp.zeros((N_KV_HEADS, V_ROWS - HEAD_DIM - 1, lk), BF16)], axis=1)
    att = _attention(q.T, k_h, vt)
    fm = _fourier(ab)
    x1 = _outproj(x2d, att, fm, w_out_hyb[0].astype(BF16), mod(0, 0, 2))
    x2 = _ffn(x1, row(g_ffn[0]), mod(0, 0, 3), mod(0, 0, 4), mod(0, 0, 5),
              w_up[0].astype(BF16), w_fdw[0], row(b_fdw[0]), w_down[0].astype(BF16))

    x3 = _conformer(x2, row(g_mix[1]), mod(1, 0, 0), mod(1, 0, 1), mod(1, 0, 2),
                    w_pw1[0].astype(BF16), row(b_pw1[0]), w_cdw[0], row(b_cdw[0]),
                    row(ln_g[0]), row(ln_b[0]), w_pw2[0].astype(BF16), row(b_pw2[0]))
    x4 = _ffn(x3, row(g_ffn[1]), mod(1, 0, 3), mod(1, 0, 4), mod(1, 0, 5),
              w_up[1].astype(BF16), w_fdw[1], row(b_fdw[1]), w_down[1].astype(BF16))
    return x4.reshape(batch, seq, d)
```

```python
import functools
import math

import numpy as np
import jax
import jax.numpy as jnp
from jax import lax
from jax.experimental import pallas as pl
from jax.experimental.pallas import tpu as pltpu

F32 = jnp.float32
BF16 = jnp.bfloat16

D_MODEL = 1024
GRID_W = 64
HEAD_DIM = 64
N_Q_HEADS = 8
N_KV_HEADS = 2
GQA_GROUP = N_Q_HEADS // N_KV_HEADS
Q_W = N_Q_HEADS * HEAD_DIM
KV_W = N_KV_HEADS * HEAD_DIM
F_GROUPS = 8
F_GROUP_DIM = 64
F_W = F_GROUPS * F_GROUP_DIM
HYB_IN = Q_W + 2 * KV_W + F_W
ROPE_HALF = HEAD_DIM // 2
ROPE_THETA = 10000.0
CONV_WIDTH = 31
FFN_DIM = 2816
NORM_EPS = 1e-6
LN_EPS = 1e-5

LANES = 128
SUBLANES = 8
FFT_N = 128
V_ROWS = 80
VMEM_LIMIT = 56 * 1024 * 1024

Q_SCALE = HEAD_DIM ** -0.5 * math.log2(math.e)


def _cparams(n_axes=1):
    return pltpu.CompilerParams(dimension_semantics=("arbitrary",) * n_axes,
                                vmem_limit_bytes=VMEM_LIMIT)


def _const_spec(shape):
    zeros = (0,) * len(shape)
    return pl.BlockSpec(shape, lambda *_: zeros, pipeline_mode=pl.Buffered(1))


def _rms_mod(x, g, shift, scale):
    ms = jnp.mean(x * x, axis=-1, keepdims=True)
    return (x * lax.rsqrt(ms + NORM_EPS) * g) * (1.0 + scale) + shift


def _silu(x):
    return x * (1.0 / (1.0 + jnp.exp(-x)))


def _ada_kernel(cond_ref, w_ref, b_ref, o_ref):
    cnd = cond_ref[...]
    o_ref[0] = jnp.dot(_silu(cnd), w_ref[0], preferred_element_type=F32,
                       precision=lax.Precision.HIGHEST) + b_ref[0]


def _ada(cond, w_ada, b_ada):
    depth, d, n = w_ada.shape
    tn = 1536
    return pl.pallas_call(
        _ada_kernel,
        grid=(depth, n // tn),
        in_specs=[pl.BlockSpec((SUBLANES, d), lambda i, j: (0, 0)),
                  pl.BlockSpec((1, d, tn), lambda i, j: (i, 0, j)),
                  pl.BlockSpec((1, 1, tn), lambda i, j: (i, 0, j))],
        out_specs=pl.BlockSpec((1, SUBLANES, tn), lambda i, j: (i, 0, j)),
        out_shape=jax.ShapeDtypeStruct((depth, SUBLANES, n), F32),
        compiler_params=_cparams(2),
        name="ada",
    )(cond, w_ada, b_ada.reshape(depth, 1, n))


def _inproj_kernel(x_ref, gm_ref, sh_ref, sc_ref, w_ref, qg_ref, kg_ref, cos_ref, sin_ref,
                   bd_ref, dft_ref, q_ref, k_ref, v_ref, ab_ref):
    h = _rms_mod(x_ref[...], gm_ref[...], sh_ref[...], sc_ref[...])
    u = jnp.dot(h.astype(BF16), w_ref[...], preferred_element_type=F32)
    cos = cos_ref[...]
    sin = sin_ref[...]
    bd = bd_ref[...]
    lane = lax.broadcasted_iota(jnp.int32, cos.shape, 1)
    first = (lane % ROPE_HALF) < (ROPE_HALF // 2)

    def head_norm_rope(t, gain):
        sq = t * t
        hi = sq.astype(BF16)
        lo = (sq - hi.astype(F32)).astype(BF16)
        ms = (jnp.dot(hi, bd, preferred_element_type=F32)
              + jnp.dot(lo, bd, preferred_element_type=F32))
        tn = t * lax.rsqrt(ms + NORM_EPS) * gain
        partner = jnp.where(first, pltpu.roll(tn, LANES - ROPE_HALF // 2, 1),
                            pltpu.roll(tn, ROPE_HALF // 2, 1))
        return tn * cos + partner * sin

    qg = qg_ref[...]
    for j in range(Q_W // LANES):
        t = head_norm_rope(u[:, j * LANES:(j + 1) * LANES], qg)
        q_ref[:, j * LANES:(j + 1) * LANES] = (t * Q_SCALE).astype(BF16)
    k_ref[...] = head_norm_rope(u[:, Q_W:Q_W + KV_W], kg_ref[...]).astype(BF16)
    v_ref[...] = u[:, Q_W + KV_W:Q_W + 2 * KV_W].astype(BF16)
    f = u[:, Q_W + 2 * KV_W:].astype(BF16)
    ab_ref[...] = jnp.dot(f, dft_ref[...], preferred_element_type=F32).astype(BF16)


def _inproj(x2d, gm, shift, scale, w_in, qg, kg, cos_t, sin_t, bd, dft):
    n = x2d.shape[0]
    tm = min(512, n)
    row = lambda w: pl.BlockSpec((tm, w), lambda i: (i, 0))
    vec = lambda w: _const_spec((1, w))
    return pl.pallas_call(
        _inproj_kernel,
        grid=(n // tm,),
        in_specs=[row(D_MODEL), vec(D_MODEL), vec(D_MODEL), vec(D_MODEL),
                  _const_spec((D_MODEL, HYB_IN)), vec(LANES), vec(LANES),
                  row(LANES), row(LANES), _const_spec((LANES, LANES)),
                  _const_spec((F_W, 2 * F_W))],
        out_specs=[row(Q_W), row(KV_W), row(KV_W), row(2 * F_W)],
        out_shape=[jax.ShapeDtypeStruct((n, Q_W), BF16), jax.ShapeDtypeStruct((n, KV_W), BF16),
                   jax.ShapeDtypeStruct((n, KV_W), BF16), jax.ShapeDtypeStruct((n, 2 * F_W), BF16)],
        compiler_params=_cparams(1),
        name="inproj",
    )(x2d, gm, shift, scale, w_in, qg, kg, cos_t, sin_t, bd, dft)


def _rope_tables(n):
    inv_freq = 1.0 / (ROPE_THETA ** (np.arange(0, ROPE_HALF, 2, dtype=np.float64) / ROPE_HALF))
    rows = n // GRID_W
    e = np.arange(ROPE_HALF)
    sign = np.where(e < ROPE_HALF // 2, -1.0, 1.0)
    ang_r = np.arange(rows)[:, None] * inv_freq[e % (ROPE_HALF // 2)][None, :]
    ang_c = np.arange(GRID_W)[:, None] * inv_freq[e % (ROPE_HALF // 2)][None, :]

    def build(fr, fc):
        tr = jnp.broadcast_to(jnp.asarray(fr, F32)[:, None, :], (rows, GRID_W, ROPE_HALF))
        tc = jnp.broadcast_to(jnp.asarray(fc, F32)[None, :, :], (rows, GRID_W, ROPE_HALF))
        t = jnp.concatenate([tr, tc], axis=-1).reshape(n, HEAD_DIM)
        return jnp.concatenate([t, t], axis=-1)

    return build(np.cos(ang_r), np.cos(ang_c)), build(np.sin(ang_r) * sign, np.sin(ang_c) * sign)


def _headnorm_matrix():
    i = np.arange(LANES)
    return jnp.asarray((i[:, None] // HEAD_DIM == i[None, :] // HEAD_DIM) / HEAD_DIM, BF16)


def _channel_dft_matrix():
    i = np.arange(F_W)
    same = (i[:, None] // F_GROUP_DIM == i[None, :] // F_GROUP_DIM)
    ang = 2.0 * np.pi * ((i[:, None] % F_GROUP_DIM) * (i[None, :] % F_GROUP_DIM) % F_GROUP_DIM) / F_GROUP_DIM
    s = F_GROUP_DIM ** -0.5
    return jnp.asarray(np.concatenate([np.cos(ang) * same * s, np.sin(ang) * same * s], axis=1), BF16)


ATTN_TQ = 256
ATTN_TK = 256
ATTN_UNROLL = 5


def _attn_kernel(qt_ref, k_ref, vt_ref, o_ref, *scr, tk, unroll):
    s_scr = scr[:unroll]
    p_scr = scr[unroll:2 * unroll]
    acc_scr = scr[2 * unroll]
    nb = k_ref.shape[0] // tk
    qt = qt_ref[...]

    def scores(blk, slot):
        off = pl.multiple_of(blk * tk, tk)
        s_scr[slot][...] = jnp.dot(k_ref[pl.ds(off, tk), :], qt, preferred_element_type=F32)

    def probs(slot):
        p_scr[slot][...] = jnp.exp2(s_scr[slot][...]).astype(BF16)

    def accumulate(blk, slot):
        off = pl.multiple_of(blk * tk, tk)
        acc_scr[...] += jnp.dot(vt_ref[:, pl.ds(off, tk)], p_scr[slot][...],
                                preferred_element_type=F32)

    def steps(j, n_probs, n_scores):
        for s in range(unroll):
            accumulate(j * unroll + s, s)
            if s < n_probs:
                probs((s + 1) % unroll)
            if s < n_scores:
                scores(j * unroll + s + 2, (s + 2) % unroll)

    def body(j, carry):
        steps(j, unroll, unroll)
        return carry

    acc_scr[...] = jnp.zeros_like(acc_scr)
    scores(0, 0)
    scores(1, 1)
    probs(0)
    n_iter = nb // unroll
    lax.fori_loop(0, n_iter - 1, body, 0)
    steps(n_iter - 1, unroll - 1, unroll - 2)
    acc = acc_scr[...]
    o = acc[:HEAD_DIM, :] * (1.0 / acc[HEAD_DIM:HEAD_DIM + 1, :])
    tq = o.shape[1] // GQA_GROUP
    outs = [o[:, g * tq:(g + 1) * tq].T for g in range(GQA_GROUP)]
    o_ref[...] = jnp.concatenate(outs, axis=1).astype(o_ref.dtype)


def _attention(qt, k, vt):
    tq, tk, unroll = ATTN_TQ, ATTN_TK, ATTN_UNROLL
    nq = qt.shape[1]
    lk = k.shape[1]
    assert lk % (tk * unroll) == 0 and lk // tk >= 2 * unroll
    gw = GQA_GROUP * HEAD_DIM
    nlane = GQA_GROUP * tq
    return pl.pallas_call(
        functools.partial(_attn_kernel, tk=tk, unroll=unroll),
        grid=(N_KV_HEADS, nq),
        in_specs=[pl.BlockSpec((None, None, HEAD_DIM, nlane), lambda h, i: (h, i, 0, 0)),
                  pl.BlockSpec((None, lk, HEAD_DIM), lambda h, i: (h, 0, 0)),
                  pl.BlockSpec((None, V_ROWS, lk), lambda h, i: (h, 0, 0))],
        out_specs=pl.BlockSpec((tq, gw), lambda h, i: (i, h)),
        out_shape=jax.ShapeDtypeStruct((nq * tq, Q_W), BF16),
        scratch_shapes=([pltpu.VMEM((tk, nlane), F32)] * unroll
                        + [pltpu.VMEM((tk, nlane), BF16)] * unroll
                        + [pltpu.VMEM((V_ROWS, nlane), F32)]),
        compiler_params=_cparams(2),
        name="attention",
    )(qt, k, vt)


def _fft1_kernel(ab_ref, ca_ref, cb_ref, tr_ref, ti_ref, *, nblk):
    ca = ca_ref[...]
    cb = cb_ref[...]
    for j in range(nblk):
        a = ab_ref[:, j * 2 * F_W:j * 2 * F_W + F_W]
        b = ab_ref[:, j * 2 * F_W + F_W:(j + 1) * 2 * F_W]
        t = (jnp.dot(ca, a, preferred_element_type=F32)
             + jnp.dot(cb, b, preferred_element_type=F32))
        tr_ref[:, j * F_W:(j + 1) * F_W] = t[:FFT_N].astype(BF16)
        ti_ref[:, j * F_W:(j + 1) * F_W] = t[FFT_N:].astype(BF16)


def _fft2_kernel(tr_ref, ti_ref, m_ref, y_ref, *, nblk):
    for j in range(nblk):
        m = m_ref[j]
        y = (jnp.dot(m[:, :FFT_N], tr_ref[j * FFT_N:(j + 1) * FFT_N, :], preferred_element_type=F32)
             + jnp.dot(m[:, FFT_N:], ti_ref[j * FFT_N:(j + 1) * FFT_N, :], preferred_element_type=F32))
        y_ref[:, j * F_W:(j + 1) * F_W] = y.astype(BF16)


def _fft_tables():
    n = FFT_N
    l = n * n
    s = n ** -0.5
    k = np.arange(n)
    ang1 = 2.0 * np.pi * ((k[:, None] * k[None, :]) % n) / n
    c1, s1 = np.cos(ang1) * s, np.sin(ang1) * s
    ca = np.concatenate([c1, -s1], axis=0)
    cb = np.concatenate([-s1, -c1], axis=0)
    k1 = k[:, None, None]
    k2 = k[None, :, None]
    n2 = k[None, None, :]
    ang2 = 2.0 * np.pi * ((n2 * (k1 + n * k2)) % l) / l
    m = np.concatenate([np.cos(ang2) * s, np.sin(ang2) * s], axis=2)
    return jnp.asarray(ca, BF16), jnp.asarray(cb, BF16), jnp.asarray(m, BF16)


def _fourier(ab):
    l = ab.shape[0]
    assert l == FFT_N * FFT_N
    ca, cb, m = _fft_tables()
    nblk = 16
    ab_v = ab.reshape(FFT_N, FFT_N * 2 * F_W)
    tr, ti = pl.pallas_call(
        functools.partial(_fft1_kernel, nblk=nblk),
        grid=(FFT_N // nblk,),
        in_specs=[pl.BlockSpec((FFT_N, nblk * 2 * F_W), lambda i: (0, i)),
                  _const_spec((2 * FFT_N, FFT_N)), _const_spec((2 * FFT_N, FFT_N))],
        out_specs=[pl.BlockSpec((FFT_N, nblk * F_W), lambda i: (0, i))] * 2,
        out_shape=[jax.ShapeDtypeStruct((FFT_N, FFT_N * F_W), BF16)] * 2,
        compiler_params=_cparams(1),
        name="fft_stage1",
    )(ab_v, ca, cb)
    tr = tr.reshape(l, F_W)
    ti = ti.reshape(l, F_W)
    y = pl.pallas_call(
        functools.partial(_fft2_kernel, nblk=nblk),
        grid=(FFT_N // nblk,),
        in_specs=[pl.BlockSpec((nblk * FFT_N, F_W), lambda i: (i, 0)),
                  pl.BlockSpec((nblk * FFT_N, F_W), lambda i: (i, 0)),
                  pl.BlockSpec((nblk, FFT_N, 2 * FFT_N), lambda i: (i, 0, 0))],
        out_specs=pl.BlockSpec((FFT_N, nblk * F_W), lambda i: (0, i)),
        out_shape=jax.ShapeDtypeStruct((FFT_N, FFT_N * F_W), BF16),
        compiler_params=_cparams(1),
        name="fft_stage2",
    )(tr, ti, m)
    return y.reshape(l, F_W)


def _outproj_kernel(x_ref, a_ref, y_ref, w_ref, gate_ref, o_ref):
    o = (jnp.dot(a_ref[...], w_ref[:Q_W, :], preferred_element_type=F32)
         + jnp.dot(y_ref[...], w_ref[Q_W:, :], preferred_element_type=F32))
    o_ref[...] = x_ref[...] + gate_ref[...] * o


def _outproj(x2d, a, y, w_out, gate):
    n = x2d.shape[0]
    tm = 512
    row = lambda w: pl.BlockSpec((tm, w), lambda i: (i, 0))
    return pl.pallas_call(
        _outproj_kernel,
        grid=(n // tm,),
        in_specs=[row(D_MODEL), row(Q_W), row(F_W), _const_spec((Q_W + F_W, D_MODEL)),
                  _const_spec((1, D_MODEL))],
        out_specs=row(D_MODEL),
        out_shape=jax.ShapeDtypeStruct((n, D_MODEL), F32),
        compiler_params=_cparams(1),
        name="outproj",
    )(x2d, a, y, w_out, gate)


FFN_HALO = SUBLANES
FFN_CHUNKS = 2


def _ffn_kernel(xp_ref, x_ref, xn_ref, g_ref, sh_ref, sc_ref, gate_ref, wup_ref, wdw_ref, bdw_ref,
                wdn_ref, o_ref):
    i = pl.program_id(0)
    last = pl.num_programs(0) - 1
    tm = x_ref.shape[0]
    g, sh, sc = g_ref[...], sh_ref[...], sc_ref[...]
    x = x_ref[...]
    hp = _rms_mod(xp_ref[...], g, sh, sc) * jnp.where(i > 0, 1.0, 0.0)
    hn = _rms_mod(xn_ref[...], g, sh, sc) * jnp.where(i < last, 1.0, 0.0)
    h = jnp.concatenate([hp, _rms_mod(x, g, sh, sc), hn], axis=0).astype(BF16)
    rows = tm + 2 * FFN_HALO
    cw = FFN_DIM // FFN_CHUNKS

    def conv(u, c0):
        w = wdw_ref[:, c0:c0 + cw]
        um = pltpu.roll(u, 1, 0)[FFN_HALO:FFN_HALO + tm]
        up = pltpu.roll(u, rows - 1, 0)[FFN_HALO:FFN_HALO + tm]
        return (um * w[0:1] + u[FFN_HALO:FFN_HALO + tm] * w[1:2] + up * w[2:3]
                + bdw_ref[:, c0:c0 + cw])

    acc = None
    for c in range(FFN_CHUNKS):
        ca, cb = c * cw, FFN_DIM + c * cw
        a = conv(jnp.dot(h, wup_ref[:, ca:ca + cw], preferred_element_type=F32), ca)
        b = conv(jnp.dot(h, wup_ref[:, cb:cb + cw], preferred_element_type=F32), cb)
        act = (_silu(a) * b).astype(BF16)
        d = jnp.dot(act, wdn_ref[ca:ca + cw, :], preferred_element_type=F32)
        acc = d if acc is None else acc + d
    o_ref[...] = x + gate_ref[...] * acc


def _ffn(x2d, g, shift, scale, gate, w_up, w_dw, b_dw, w_down):
    n = x2d.shape[0]
    tm = 512
    hb = tm // FFN_HALO
    nh = n // FFN_HALO
    vec = lambda: _const_spec((1, D_MODEL))
    return pl.pallas_call(
        _ffn_kernel,
        grid=(n // tm,),
        in_specs=[pl.BlockSpec((FFN_HALO, D_MODEL), lambda i: (jnp.maximum(i * hb - 1, 0), 0)),
                  pl.BlockSpec((tm, D_MODEL), lambda i: (i, 0)),
                  pl.BlockSpec((FFN_HALO, D_MODEL), lambda i: (jnp.minimum((i + 1) * hb, nh - 1), 0)),
                  vec(), vec(), vec(), vec(),
                  _const_spec((D_MODEL, 2 * FFN_DIM)), _const_spec((3, 2 * FFN_DIM)),
                  _const_spec((1, 2 * FFN_DIM)), _const_spec((FFN_DIM, D_MODEL))],
        out_specs=pl.BlockSpec((tm, D_MODEL), lambda i: (i, 0)),
        out_shape=jax.ShapeDtypeStruct((n, D_MODEL), F32),
        compiler_params=_cparams(1),
        name="ffn",
    )(x2d, x2d, x2d, g, shift, scale, gate, w_up, w_dw, b_dw, w_down)


CONF_HALO = 2 * SUBLANES


def _conf_kernel(xp_ref, x_ref, xn_ref, g_ref, sh_ref, sc_ref, gate_ref, w1_ref, b1_ref, wdw_ref,
                 bdw_ref, lng_ref, lnb_ref, w2_ref, b2_ref, o_ref, u_scr, *, seq_len):
    i = pl.program_id(0)
    tm = x_ref.shape[0]
    g, sh, sc = g_ref[...], sh_ref[...], sc_ref[...]
    x = x_ref[...]
    h = jnp.concatenate([_rms_mod(xp_ref[...], g, sh, sc), _rms_mod(x, g, sh, sc),
                         _rms_mod(xn_ref[...], g, sh, sc)], axis=0).astype(BF16)
    u = jnp.dot(h, w1_ref[...], preferred_element_type=F32) + b1_ref[...]
    glu = u[:, :D_MODEL] * (1.0 / (1.0 + jnp.exp(-u[:, D_MODEL:])))
    pos = i * tm - CONF_HALO + lax.broadcasted_iota(jnp.int32, (tm + 2 * CONF_HALO, 1), 0)
    u_scr[...] = jnp.where((pos >= 0) & (pos < seq_len), glu, 0.0)
    pad = (CONV_WIDTH - 1) // 2
    acc = jnp.zeros((tm, D_MODEL), F32) + bdw_ref[...]
    for k in range(CONV_WIDTH):
        acc = acc + u_scr[pl.ds(CONF_HALO - pad + k, tm), :] * wdw_ref[k:k + 1, :]
    mu = jnp.mean(acc, axis=-1, keepdims=True)
    xc = acc - mu
    var = jnp.mean(xc * xc, axis=-1, keepdims=True)
    y = xc * lax.rsqrt(var + LN_EPS) * lng_ref[...] + lnb_ref[...]
    o = jnp.dot(_silu(y).astype(BF16), w2_ref[...], preferred_element_type=F32) + b2_ref[...]
    o_ref[...] = x + gate_ref[...] * o


def _conformer(x2d, g, shift, scale, gate, w1, b1, wdw, bdw, lng, lnb, w2, b2):
    n = x2d.shape[0]
    tm = 512
    hb = tm // CONF_HALO
    nh = n // CONF_HALO
    vec = lambda w=D_MODEL: _const_spec((1, w))
    return pl.pallas_call(
        functools.partial(_conf_kernel, seq_len=n),
        grid=(n // tm,),
        in_specs=[pl.BlockSpec((CONF_HALO, D_MODEL), lambda i: (jnp.maximum(i * hb - 1, 0), 0)),
                  pl.BlockSpec((tm, D_MODEL), lambda i: (i, 0)),
                  pl.BlockSpec((CONF_HALO, D_MODEL), lambda i: (jnp.minimum((i + 1) * hb, nh - 1), 0)),
                  vec(), vec(), vec(), vec(),
                  _const_spec((D_MODEL, 2 * D_MODEL)), vec(2 * D_MODEL),
                  _const_spec((CONV_WIDTH, D_MODEL)), vec(), vec(), vec(),
                  _const_spec((D_MODEL, D_MODEL)), vec()],
        out_specs=pl.BlockSpec((tm, D_MODEL), lambda i: (i, 0)),
        out_shape=jax.ShapeDtypeStruct((n, D_MODEL), F32),
        scratch_shapes=[pltpu.VMEM((tm + 2 * CONF_HALO, D_MODEL), F32)],
        compiler_params=_cparams(1),
        name="conformer",
    )(x2d, x2d, x2d, g, shift, scale, gate, w1, b1, wdw, bdw, lng, lnb, w2, b2)


def kernel(x, c, ctx, c_ctx, w_ada, b_ada, g_mix, g_ffn, w_in_hyb, q_gain, k_gain, w_out_hyb,
           w_pw1, b_pw1, w_cdw, b_cdw, ln_g, ln_b, w_pw2, b_pw2, w_up, w_fdw, b_fdw, w_down):
    batch, seq, d = x.shape
    assert batch == 1 and d == D_MODEL
    x2d = x.reshape(seq, d)
    ctx2d = ctx.reshape(-1, d)
    n_ctx = ctx2d.shape[0]
    row = lambda v: v.reshape(1, -1)

    cond = jnp.zeros((SUBLANES, d), F32).at[0].set(c[0]).at[1].set(c_ctx)
    mods = _ada(cond, w_ada, b_ada)
    mod = lambda layer, who, j: mods[layer, who:who + 1, j * d:(j + 1) * d]

    w_in = w_in_hyb[0].astype(BF16)
    qg = row(jnp.tile(q_gain[0], LANES // HEAD_DIM))
    kg = row(jnp.tile(k_gain[0], LANES // HEAD_DIM))
    bd = _headnorm_matrix()
    dft = _channel_dft_matrix()
    cos_t, sin_t = _rope_tables(seq)
    gm = row(g_mix[0])
    q, k_l, v_l, ab = _inproj(x2d, gm, mod(0, 0, 0), mod(0, 0, 1), w_in, qg, kg, cos_t, sin_t, bd, dft)
    ones_c = jnp.ones((n_ctx, LANES), F32)
    _, k_c, v_c, _ = _inproj(ctx2d, gm, mod(0, 1, 0), mod(0, 1, 1), w_in, qg, kg, ones_c,
                             jnp.zeros_like(ones_c), bd, dft)
    k_all = jnp.concatenate([k_l, k_c], axis=0)
    v_all = jnp.concatenate([v_l, v_c], axis=0)
    lk = k_all.shape[0]
    k_h = k_all.reshape(lk, N_KV_HEADS, HEAD_DIM).transpose(1, 0, 2)
    vt = v_all.reshape(lk, N_KV_HEADS, HEAD_DIM).transpose(1, 2, 0)
    vt = jnp.concatenate([vt, jnp.ones((N_KV_HEADS, 1, lk), BF16),
                          jnp.zeros((N_KV_HEADS, V_ROWS - HEAD_DIM - 1, lk), BF16)], axis=1)
    qt = q.reshape(seq // ATTN_TQ, ATTN_TQ, N_KV_HEADS, GQA_GROUP, HEAD_DIM)
    qt = qt.transpose(2, 0, 4, 3, 1).reshape(N_KV_HEADS, seq // ATTN_TQ, HEAD_DIM, GQA_GROUP * ATTN_TQ)
    att = _attention(qt, k_h, vt)
    fm = _fourier(ab)
    x1 = _outproj(x2d, att, fm, w_out_hyb[0].astype(BF16), mod(0, 0, 2))
    x2 = _ffn(x1, row(g_ffn[0]), mod(0, 0, 3), mod(0, 0, 4), mod(0, 0, 5),
              w_up[0].astype(BF16), w_fdw[0], row(b_fdw[0]), w_down[0].astype(BF16))

    x3 = _conformer(x2, row(g_mix[1]), mod(1, 0, 0), mod(1, 0, 1), mod(1, 0, 2),
                    w_pw1[0].astype(BF16), row(b_pw1[0]), w_cdw[0], row(b_cdw[0]),
                    row(ln_g[0]), row(ln_b[0]), w_pw2[0].astype(BF16), row(b_pw2[0]))
    x4 = _ffn(x3, row(g_ffn[1]), mod(1, 0, 3), mod(1, 0, 4), mod(1, 0, 5),
              w_up[1].astype(BF16), w_fdw[1], row(b_fdw[1]), w_down[1].astype(BF16))
    return x4.reshape(batch, seq, d)
```

```python
import functools
import math

import numpy as np
import jax
import jax.numpy as jnp
from jax import lax
from jax.experimental import pallas as pl
from jax.experimental.pallas import tpu as pltpu

F32 = jnp.float32
BF16 = jnp.bfloat16

D_MODEL = 1024
GRID_W = 64
HEAD_DIM = 64
N_Q_HEADS = 8
N_KV_HEADS = 2
GQA_GROUP = N_Q_HEADS // N_KV_HEADS
Q_W = N_Q_HEADS * HEAD_DIM
KV_W = N_KV_HEADS * HEAD_DIM
F_GROUPS = 8
F_GROUP_DIM = 64
F_W = F_GROUPS * F_GROUP_DIM
HYB_IN = Q_W + 2 * KV_W + F_W
ROPE_HALF = HEAD_DIM // 2
ROPE_THETA = 10000.0
CONV_WIDTH = 31
FFN_DIM = 2816
NORM_EPS = 1e-6
LN_EPS = 1e-5

LANES = 128
SUBLANES = 8
FFT_N = 128
V_ROWS = 80
VMEM_LIMIT = 56 * 1024 * 1024

Q_SCALE = HEAD_DIM ** -0.5 * math.log2(math.e)


def _cparams(n_axes=1):
    return pltpu.CompilerParams(dimension_semantics=("arbitrary",) * n_axes,
                                vmem_limit_bytes=VMEM_LIMIT)


def _const_spec(shape):
    zeros = (0,) * len(shape)
    return pl.BlockSpec(shape, lambda *_: zeros, pipeline_mode=pl.Buffered(1))


def _rms_mod(x, g, shift, scale):
    ms = jnp.mean(x * x, axis=-1, keepdims=True)
    return (x * lax.rsqrt(ms + NORM_EPS) * g) * (1.0 + scale) + shift


def _silu(x):
    return x * (1.0 / (1.0 + jnp.exp(-x)))


def _ada_kernel(cond_ref, w_ref, b_ref, o_ref):
    cnd = cond_ref[...]
    o_ref[0] = jnp.dot(_silu(cnd), w_ref[0], preferred_element_type=F32,
                       precision=lax.Precision.HIGHEST) + b_ref[0]


def _ada(cond, w_ada, b_ada):
    depth, d, n = w_ada.shape
    tn = 1536
    return pl.pallas_call(
        _ada_kernel,
        grid=(depth, n // tn),
        in_specs=[pl.BlockSpec((SUBLANES, d), lambda i, j: (0, 0)),
                  pl.BlockSpec((1, d, tn), lambda i, j: (i, 0, j)),
                  pl.BlockSpec((1, 1, tn), lambda i, j: (i, 0, j))],
        out_specs=pl.BlockSpec((1, SUBLANES, tn), lambda i, j: (i, 0, j)),
        out_shape=jax.ShapeDtypeStruct((depth, SUBLANES, n), F32),
        compiler_params=_cparams(2),
        name="ada",
    )(cond, w_ada, b_ada.reshape(depth, 1, n))


INPROJ_TM = ATTN_TQ = 256
GRID_ROWS_PER_TILE = INPROJ_TM // GRID_W


def _inproj_kernel(x_ref, ctx_ref, gm_ref, sh_ref, sc_ref, w_ref, qg_ref, kg_ref, rcos_ref, rsin_ref,
                   ccos_ref, csin_ref, bd_ref, dft_ref, qt_ref, k_ref, vt_ref, ab_ref, *, n_lat):
    i = pl.program_id(0)
    is_ctx = i >= n_lat
    xin = jnp.where(is_ctx, ctx_ref[...], x_ref[...])
    h = _rms_mod(xin, gm_ref[...], sh_ref[...], sc_ref[...])
    u = jnp.dot(h.astype(BF16), w_ref[...], preferred_element_type=F32)

    def rope_table(row_ref, col_ref, ctx_value):
        rows = [jnp.tile(row_ref[g * SUBLANES:(g + 1) * SUBLANES, :], (GRID_W // SUBLANES, 1))
                for g in range(GRID_ROWS_PER_TILE)]
        t = jnp.concatenate(rows, axis=0) + jnp.tile(col_ref[...], (GRID_ROWS_PER_TILE, 1))
        return jnp.where(is_ctx, ctx_value, t)

    cos = rope_table(rcos_ref, ccos_ref, 1.0)
    sin = rope_table(rsin_ref, csin_ref, 0.0)
    bd = bd_ref[...]
    lane = lax.broadcasted_iota(jnp.int32, cos.shape, 1)
    first = (lane % ROPE_HALF) < (ROPE_HALF // 2)

    def head_norm_rope(t, gain):
        sq = t * t
        hi = sq.astype(BF16)
        lo = (sq - hi.astype(F32)).astype(BF16)
        ms = (jnp.dot(hi, bd, preferred_element_type=F32)
              + jnp.dot(lo, bd, preferred_element_type=F32))
        tn = t * lax.rsqrt(ms + NORM_EPS) * gain
        partner = jnp.where(first, pltpu.roll(tn, LANES - ROPE_HALF // 2, 1),
                            pltpu.roll(tn, ROPE_HALF // 2, 1))
        return tn * cos + partner * sin

    kk = head_norm_rope(u[:, Q_W:Q_W + KV_W], kg_ref[...]).astype(BF16)
    vt = u[:, Q_W + KV_W:Q_W + 2 * KV_W].T
    pad_rows = lax.broadcasted_iota(jnp.int32, (V_ROWS - HEAD_DIM, INPROJ_TM), 0)
    ones_rows = jnp.where(pad_rows == 0, 1.0, 0.0).astype(BF16)
    for kv in range(N_KV_HEADS):
        k_ref[kv] = kk[:, kv * HEAD_DIM:(kv + 1) * HEAD_DIM]
        vt_ref[kv, :HEAD_DIM, :] = vt[kv * HEAD_DIM:(kv + 1) * HEAD_DIM, :].astype(BF16)
        vt_ref[kv, HEAD_DIM:, :] = ones_rows

    @pl.when(i < n_lat)
    def _():
        qg = qg_ref[...]
        for j in range(Q_W // LANES):
            t = head_norm_rope(u[:, j * LANES:(j + 1) * LANES], qg) * Q_SCALE
            tt = t.T.astype(BF16)
            for hh in range(LANES // HEAD_DIM):
                head = j * (LANES // HEAD_DIM) + hh
                kv, g = head // GQA_GROUP, head % GQA_GROUP
                qt_ref[kv, :, g * INPROJ_TM:(g + 1) * INPROJ_TM] = tt[hh * HEAD_DIM:(hh + 1) * HEAD_DIM, :]
        f = u[:, Q_W + 2 * KV_W:].astype(BF16)
        ab_ref[...] = jnp.dot(f, dft_ref[...], preferred_element_type=F32).astype(BF16)


def _inproj(x2d, ctx2d, gm, shift2, scale2, w_in, qg, kg, rope, bd, dft):
    n = x2d.shape[0]
    tm = INPROJ_TM
    n_lat = n // tm
    assert ctx2d.shape[0] == tm and n % tm == 0
    lk = n + tm
    lat = lambda i: jnp.minimum(i, n_lat - 1)
    vec = lambda w: _const_spec((1, w))
    mod = pl.BlockSpec((None, 1, D_MODEL), lambda i: (i // n_lat, 0, 0))
    rtab = pl.BlockSpec((GRID_ROWS_PER_TILE * SUBLANES, LANES), lambda i: (lat(i), 0))
    return pl.pallas_call(
        functools.partial(_inproj_kernel, n_lat=n_lat),
        grid=(n_lat + 1,),
        in_specs=[pl.BlockSpec((tm, D_MODEL), lambda i: (lat(i), 0)), _const_spec((tm, D_MODEL)),
                  vec(D_MODEL), mod, mod, _const_spec((D_MODEL, HYB_IN)), vec(LANES), vec(LANES),
                  rtab, rtab, _const_spec((GRID_W, LANES)), _const_spec((GRID_W, LANES)),
                  _const_spec((LANES, LANES)), _const_spec((F_W, 2 * F_W))],
        out_specs=[pl.BlockSpec((N_KV_HEADS, None, HEAD_DIM, GQA_GROUP * tm), lambda i: (0, lat(i), 0, 0)),
                   pl.BlockSpec((N_KV_HEADS, tm, HEAD_DIM), lambda i: (0, i, 0)),
                   pl.BlockSpec((N_KV_HEADS, V_ROWS, tm), lambda i: (0, 0, i)),
                   pl.BlockSpec((tm, 2 * F_W), lambda i: (lat(i), 0))],
        out_shape=[jax.ShapeDtypeStruct((N_KV_HEADS, n_lat, HEAD_DIM, GQA_GROUP * tm), BF16),
                   jax.ShapeDtypeStruct((N_KV_HEADS, lk, HEAD_DIM), BF16),
                   jax.ShapeDtypeStruct((N_KV_HEADS, V_ROWS, lk), BF16),
                   jax.ShapeDtypeStruct((n, 2 * F_W), BF16)],
        compiler_params=_cparams(1),
        name="inproj",
    )(x2d, ctx2d, gm, shift2, scale2, w_in, qg, kg, *rope, bd, dft)


def _rope_tables(n):
    inv_freq = 1.0 / (ROPE_THETA ** (np.arange(0, ROPE_HALF, 2, dtype=np.float64) / ROPE_HALF))
    rows = n // GRID_W
    e = np.arange(ROPE_HALF)
    sign = np.where(e < ROPE_HALF // 2, -1.0, 1.0)
    ang_r = np.arange(rows)[:, None] * inv_freq[e % (ROPE_HALF // 2)][None, :]
    ang_c = np.arange(GRID_W)[:, None] * inv_freq[e % (ROPE_HALF // 2)][None, :]

    def two_heads(row_half, col_half):
        t = np.concatenate([row_half, col_half], axis=1)
        return np.concatenate([t, t], axis=1)

    zr, zc = np.zeros_like(ang_r), np.zeros_like(ang_c)
    rep = lambda t: jnp.asarray(np.repeat(t, SUBLANES, axis=0), F32)
    return (rep(two_heads(np.cos(ang_r), zr)), rep(two_heads(np.sin(ang_r) * sign, zr)),
            jnp.asarray(two_heads(zc, np.cos(ang_c)), F32),
            jnp.asarray(two_heads(zc, np.sin(ang_c) * sign), F32))


def _headnorm_matrix():
    i = np.arange(LANES)
    return jnp.asarray((i[:, None] // HEAD_DIM == i[None, :] // HEAD_DIM) / HEAD_DIM, BF16)


def _channel_dft_matrix():
    i = np.arange(F_W)
    same = (i[:, None] // F_GROUP_DIM == i[None, :] // F_GROUP_DIM)
    ang = 2.0 * np.pi * ((i[:, None] % F_GROUP_DIM) * (i[None, :] % F_GROUP_DIM) % F_GROUP_DIM) / F_GROUP_DIM
    s = F_GROUP_DIM ** -0.5
    return jnp.asarray(np.concatenate([np.cos(ang) * same * s, np.sin(ang) * same * s], axis=1), BF16)


ATTN_TK = 256
ATTN_UNROLL = 13
ATTN_SLOT = tuple([0, 1, 2] * (ATTN_UNROLL // 3) + [3])
ATTN_NSLOT = max(ATTN_SLOT) + 1


def _attn_kernel(qt_ref, k_ref, vt_ref, o_ref, *scr, tk, unroll):
    s_scr = scr[:ATTN_NSLOT]
    p_scr = scr[ATTN_NSLOT:2 * ATTN_NSLOT]
    acc_scr = scr[2 * ATTN_NSLOT]
    slot_of = lambda s: ATTN_SLOT[s % unroll]
    nb = k_ref.shape[0] // tk
    qt = qt_ref[...]

    def scores(blk, slot):
        off = pl.multiple_of(blk * tk, tk)
        s_scr[slot][...] = jnp.dot(k_ref[pl.ds(off, tk), :], qt, preferred_element_type=F32)

    def probs(slot):
        p_scr[slot][...] = jnp.exp2(s_scr[slot][...]).astype(BF16)

    def accumulate(blk, slot):
        off = pl.multiple_of(blk * tk, tk)
        acc_scr[...] += jnp.dot(vt_ref[:, pl.ds(off, tk)], p_scr[slot][...],
                                preferred_element_type=F32)

    def steps(j, n_probs, n_scores):
        for s in range(unroll):
            accumulate(j * unroll + s, slot_of(s))
            if s < n_probs:
                probs(slot_of(s + 1))
            if s < n_scores:
                scores(j * unroll + s + 2, slot_of(s + 2))

    def body(j, carry):
        steps(j, unroll, unroll)
        return carry

    acc_scr[...] = jnp.zeros_like(acc_scr)
    scores(0, slot_of(0))
    scores(1, slot_of(1))
    probs(slot_of(0))
    n_iter = nb // unroll
    lax.fori_loop(0, n_iter - 1, body, 0)
    steps(n_iter - 1, unroll - 1, unroll - 2)
    acc = acc_scr[...]
    o = acc[:HEAD_DIM, :] * (1.0 / acc[HEAD_DIM:HEAD_DIM + 1, :])
    tq = o.shape[1] // GQA_GROUP
    outs = [o[:, g * tq:(g + 1) * tq].T for g in range(GQA_GROUP)]
    o_ref[...] = jnp.concatenate(outs, axis=1).astype(o_ref.dtype)


def _attention(qt, k, vt):
    tq, tk, unroll = ATTN_TQ, ATTN_TK, ATTN_UNROLL
    nq = qt.shape[1]
    lk = k.shape[1]
    assert lk % (tk * unroll) == 0 and lk // tk >= 2 * unroll
    gw = GQA_GROUP * HEAD_DIM
    nlane = GQA_GROUP * tq
    return pl.pallas_call(
        functools.partial(_attn_kernel, tk=tk, unroll=unroll),
        grid=(N_KV_HEADS, nq),
        in_specs=[pl.BlockSpec((None, None, HEAD_DIM, nlane), lambda h, i: (h, i, 0, 0)),
                  pl.BlockSpec((None, lk, HEAD_DIM), lambda h, i: (h, 0, 0)),
                  pl.BlockSpec((None, V_ROWS, lk), lambda h, i: (h, 0, 0))],
        out_specs=pl.BlockSpec((tq, gw), lambda h, i: (i, h)),
        out_shape=jax.ShapeDtypeStruct((nq * tq, Q_W), BF16),
        scratch_shapes=([pltpu.VMEM((tk, nlane), F32)] * ATTN_NSLOT
                        + [pltpu.VMEM((tk, nlane), BF16)] * ATTN_NSLOT
                        + [pltpu.VMEM((V_ROWS, nlane), F32)]),
        compiler_params=_cparams(2),
        name="attention",
    )(qt, k, vt)


def _fft1_kernel(ab_ref, ca_ref, cb_ref, tr_ref, ti_ref, *, nblk):
    ca = ca_ref[...]
    cb = cb_ref[...]
    for j in range(nblk):
        a = ab_ref[:, j * 2 * F_W:j * 2 * F_W + F_W]
        b = ab_ref[:, j * 2 * F_W + F_W:(j + 1) * 2 * F_W]
        t = (jnp.dot(ca, a, preferred_element_type=F32)
             + jnp.dot(cb, b, preferred_element_type=F32))
        tr_ref[:, j * F_W:(j + 1) * F_W] = t[:FFT_N].astype(BF16)
        ti_ref[:, j * F_W:(j + 1) * F_W] = t[FFT_N:].astype(BF16)


def _fft2_kernel(tr_ref, ti_ref, m_ref, y_ref, *, nblk):
    for j in range(nblk):
        m = m_ref[j]
        y = (jnp.dot(m[:, :FFT_N], tr_ref[j * FFT_N:(j + 1) * FFT_N, :], preferred_element_type=F32)
             + jnp.dot(m[:, FFT_N:], ti_ref[j * FFT_N:(j + 1) * FFT_N, :], preferred_element_type=F32))
        y_ref[:, j * F_W:(j + 1) * F_W] = y.astype(BF16)


def _fft_tables():
    n = FFT_N
    l = n * n
    s = n ** -0.5
    k = np.arange(n)
    ang1 = 2.0 * np.pi * ((k[:, None] * k[None, :]) % n) / n
    c1, s1 = np.cos(ang1) * s, np.sin(ang1) * s
    ca = np.concatenate([c1, -s1], axis=0)
    cb = np.concatenate([-s1, -c1], axis=0)
    k1 = k[:, None, None]
    k2 = k[None, :, None]
    n2 = k[None, None, :]
    ang2 = 2.0 * np.pi * ((n2 * (k1 + n * k2)) % l) / l
    m = np.concatenate([np.cos(ang2) * s, np.sin(ang2) * s], axis=2)
    return jnp.asarray(ca, BF16), jnp.asarray(cb, BF16), jnp.asarray(m, BF16)


def _fourier(ab):
    l = ab.shape[0]
    assert l == FFT_N * FFT_N
    ca, cb, m = _fft_tables()
    nblk = 16
    ab_v = ab.reshape(FFT_N, FFT_N * 2 * F_W)
    tr, ti = pl.pallas_call(
        functools.partial(_fft1_kernel, nblk=nblk),
        grid=(FFT_N // nblk,),
        in_specs=[pl.BlockSpec((FFT_N, nblk * 2 * F_W), lambda i: (0, i)),
                  _const_spec((2 * FFT_N, FFT_N)), _const_spec((2 * FFT_N, FFT_N))],
        out_specs=[pl.BlockSpec((FFT_N, nblk * F_W), lambda i: (0, i))] * 2,
        out_shape=[jax.ShapeDtypeStruct((FFT_N, FFT_N * F_W), BF16)] * 2,
        compiler_params=_cparams(1),
        name="fft_stage1",
    )(ab_v, ca, cb)
    tr = tr.reshape(l, F_W)
    ti = ti.reshape(l, F_W)
    y = pl.pallas_call(
        functools.partial(_fft2_kernel, nblk=nblk),
        grid=(FFT_N // nblk,),
        in_specs=[pl.BlockSpec((nblk * FFT_N, F_W), lambda i: (i, 0)),
                  pl.BlockSpec((nblk * FFT_N, F_W), lambda i: (i, 0)),
                  pl.BlockSpec((nblk, FFT_N, 2 * FFT_N), lambda i: (i, 0, 0))],
        out_specs=pl.BlockSpec((FFT_N, nblk * F_W), lambda i: (0, i)),
        out_shape=jax.ShapeDtypeStruct((FFT_N, FFT_N * F_W), BF16),
        compiler_params=_cparams(1),
        name="fft_stage2",
    )(tr, ti, m)
    return y.reshape(l, F_W)


def _outproj_kernel(x_ref, a_ref, y_ref, w_ref, gate_ref, o_ref):
    o = (jnp.dot(a_ref[...], w_ref[:Q_W, :], preferred_element_type=F32)
         + jnp.dot(y_ref[...], w_ref[Q_W:, :], preferred_element_type=F32))
    o_ref[...] = x_ref[...] + gate_ref[...] * o


def _outproj(x2d, a, y, w_out, gate):
    n = x2d.shape[0]
    tm = 512
    row = lambda w: pl.BlockSpec((tm, w), lambda i: (i, 0))
    return pl.pallas_call(
        _outproj_kernel,
        grid=(n // tm,),
        in_specs=[row(D_MODEL), row(Q_W), row(F_W), _const_spec((Q_W + F_W, D_MODEL)),
                  _const_spec((1, D_MODEL))],
        out_specs=row(D_MODEL),
        out_shape=jax.ShapeDtypeStruct((n, D_MODEL), F32),
        compiler_params=_cparams(1),
        name="outproj",
    )(x2d, a, y, w_out, gate)


FFN_HALO = SUBLANES
FFN_CHUNKS = 2


def _ffn_kernel(xp_ref, x_ref, xn_ref, g_ref, sh_ref, sc_ref, gate_ref, wup_ref, wdw_ref, bdw_ref,
                wdn_ref, o_ref):
    i = pl.program_id(0)
    last = pl.num_programs(0) - 1
    tm = x_ref.shape[0]
    g, sh, sc = g_ref[...], sh_ref[...], sc_ref[...]
    x = x_ref[...]
    hp = _rms_mod(xp_ref[...], g, sh, sc) * jnp.where(i > 0, 1.0, 0.0)
    hn = _rms_mod(xn_ref[...], g, sh, sc) * jnp.where(i < last, 1.0, 0.0)
    h = jnp.concatenate([hp, _rms_mod(x, g, sh, sc), hn], axis=0).astype(BF16)
    rows = tm + 2 * FFN_HALO
    cw = FFN_DIM // FFN_CHUNKS

    def conv(u, c0):
        w = wdw_ref[:, c0:c0 + cw]
        um = pltpu.roll(u, 1, 0)[FFN_HALO:FFN_HALO + tm]
        up = pltpu.roll(u, rows - 1, 0)[FFN_HALO:FFN_HALO + tm]
        return (um * w[0:1] + u[FFN_HALO:FFN_HALO + tm] * w[1:2] + up * w[2:3]
                + bdw_ref[:, c0:c0 + cw])

    acc = None
    for c in range(FFN_CHUNKS):
        ca, cb = c * cw, FFN_DIM + c * cw
        a = conv(jnp.dot(h, wup_ref[:, ca:ca + cw], preferred_element_type=F32), ca)
        b = conv(jnp.dot(h, wup_ref[:, cb:cb + cw], preferred_element_type=F32), cb)
        act = (_silu(a) * b).astype(BF16)
        d = jnp.dot(act, wdn_ref[ca:ca + cw, :], preferred_element_type=F32)
        acc = d if acc is None else acc + d
    o_ref[...] = x + gate_ref[...] * acc


def _ffn(x2d, g, shift, scale, gate, w_up, w_dw, b_dw, w_down):
    n = x2d.shape[0]
    tm = 512
    hb = tm // FFN_HALO
    nh = n // FFN_HALO
    vec = lambda: _const_spec((1, D_MODEL))
    return pl.pallas_call(
        _ffn_kernel,
        grid=(n // tm,),
        in_specs=[pl.BlockSpec((FFN_HALO, D_MODEL), lambda i: (jnp.maximum(i * hb - 1, 0), 0)),
                  pl.BlockSpec((tm, D_MODEL), lambda i: (i, 0)),
                  pl.BlockSpec((FFN_HALO, D_MODEL), lambda i: (jnp.minimum((i + 1) * hb, nh - 1), 0)),
                  vec(), vec(), vec(), vec(),
                  _const_spec((D_MODEL, 2 * FFN_DIM)), _const_spec((3, 2 * FFN_DIM)),
                  _const_spec((1, 2 * FFN_DIM)), _const_spec((FFN_DIM, D_MODEL))],
        out_specs=pl.BlockSpec((tm, D_MODEL), lambda i: (i, 0)),
        out_shape=jax.ShapeDtypeStruct((n, D_MODEL), F32),
        compiler_params=_cparams(1),
        name="ffn",
    )(x2d, x2d, x2d, g, shift, scale, gate, w_up, w_dw, b_dw, w_down)


CONF_HALO = 2 * SUBLANES
CONF_TM = 512
CONF_N = CONF_TM + 2 * CONF_HALO
CONF_NF = 288


def _conv_dft_tables():
    n, nf = CONF_N, CONF_N // 2 + 1
    f = np.arange(CONF_NF)[:, None]
    live = f < nf
    t = np.arange(n)[None, :]
    ang = 2.0 * np.pi * ((f * t) % n) / n
    fwd = np.concatenate([np.cos(ang) * live, np.sin(ang) * live], axis=0)
    r = np.arange(CONF_HALO, CONF_HALO + CONF_TM)[:, None]
    fi = f.T
    weight = np.where((fi == 0) | (fi == n // 2), 1.0, 2.0) * live.T / n
    angi = 2.0 * np.pi * ((r * fi) % n) / n
    inv = np.concatenate([weight * np.cos(angi), -weight * np.sin(angi)], axis=1)
    k = np.arange(CONV_WIDTH)[None, :] - (CONV_WIDTH - 1) // 2
    angw = 2.0 * np.pi * ((f * k) % n) / n
    gw = np.concatenate([np.cos(angw) * live, np.sin(angw) * live], axis=0)
    return jnp.asarray(fwd, BF16), jnp.asarray(inv, BF16), jnp.asarray(gw, F32)


def _filter_spectrum_kernel(gw_ref, w_ref, o_ref):
    o_ref[...] = jnp.dot(gw_ref[...], w_ref[...], preferred_element_type=F32,
                         precision=lax.Precision.HIGHEST)


def _filter_spectrum(gw, wdw):
    return pl.pallas_call(
        _filter_spectrum_kernel,
        out_shape=jax.ShapeDtypeStruct((2 * CONF_NF, D_MODEL), F32),
        name="conv_filter_spectrum",
    )(gw, wdw)


def _conf_kernel(xp_ref, x_ref, xn_ref, g_ref, sh_ref, sc_ref, gate_ref, w1_ref, b1_ref, fwd_ref, inv_ref,
                 gs_ref, bdw_ref, lng_ref, lnb_ref, w2_ref, b2_ref, o_ref, *, seq_len):
    i = pl.program_id(0)
    tm = x_ref.shape[0]
    g, sh, sc = g_ref[...], sh_ref[...], sc_ref[...]
    x = x_ref[...]
    h = jnp.concatenate([_rms_mod(xp_ref[...], g, sh, sc), _rms_mod(x, g, sh, sc),
                         _rms_mod(xn_ref[...], g, sh, sc)], axis=0).astype(BF16)
    u = jnp.dot(h, w1_ref[...], preferred_element_type=F32) + b1_ref[...]
    glu = u[:, :D_MODEL] * (1.0 / (1.0 + jnp.exp(-u[:, D_MODEL:])))
    pos = i * tm - CONF_HALO + lax.broadcasted_iota(jnp.int32, (CONF_N, 1), 0)
    glu = jnp.where((pos >= 0) & (pos < seq_len), glu, 0.0).astype(BF16)
    spec = jnp.dot(fwd_ref[...], glu, preferred_element_type=F32)
    uc, us = spec[:CONF_NF], spec[CONF_NF:]
    gc, gs = gs_ref[:CONF_NF, :], gs_ref[CONF_NF:, :]
    y = jnp.concatenate([uc * gc + us * gs, uc * gs - us * gc], axis=0).astype(BF16)
    acc = jnp.dot(inv_ref[...], y, preferred_element_type=F32) + bdw_ref[...]
    mu = jnp.mean(acc, axis=-1, keepdims=True)
    xc = acc - mu
    var = jnp.mean(xc * xc, axis=-1, keepdims=True)
    yn = xc * lax.rsqrt(var + LN_EPS) * lng_ref[...] + lnb_ref[...]
    o = jnp.dot(_silu(yn).astype(BF16), w2_ref[...], preferred_element_type=F32) + b2_ref[...]
    o_ref[...] = x + gate_ref[...] * o


def _conformer(x2d, g, shift, scale, gate, w1, b1, wdw, bdw, lng, lnb, w2, b2):
    n = x2d.shape[0]
    tm = CONF_TM
    hb = tm // CONF_HALO
    nh = n // CONF_HALO
    fwd, inv, gw = _conv_dft_tables()
    gspec = _filter_spectrum(gw, wdw)
    vec = lambda w=D_MODEL: _const_spec((1, w))
    return pl.pallas_call(
        functools.partial(_conf_kernel, seq_len=n),
        grid=(n // tm,),
        in_specs=[pl.BlockSpec((CONF_HALO, D_MODEL), lambda i: (jnp.maximum(i * hb - 1, 0), 0)),
                  pl.BlockSpec((tm, D_MODEL), lambda i: (i, 0)),
                  pl.BlockSpec((CONF_HALO, D_MODEL), lambda i: (jnp.minimum((i + 1) * hb, nh - 1), 0)),
                  vec(), vec(), vec(), vec(),
                  _const_spec((D_MODEL, 2 * D_MODEL)), vec(2 * D_MODEL),
                  _const_spec((2 * CONF_NF, CONF_N)), _const_spec((tm, 2 * CONF_NF)),
                  _const_spec((2 * CONF_NF, D_MODEL)), vec(), vec(), vec(),
                  _const_spec((D_MODEL, D_MODEL)), vec()],
        out_specs=pl.BlockSpec((tm, D_MODEL), lambda i: (i, 0)),
        out_shape=jax.ShapeDtypeStruct((n, D_MODEL), F32),
        compiler_params=_cparams(1),
        name="conformer",
    )(x2d, x2d, x2d, g, shift, scale, gate, w1, b1, fwd, inv, gspec, bdw, lng, lnb, w2, b2)


def kernel(x, c, ctx, c_ctx, w_ada, b_ada, g_mix, g_ffn, w_in_hyb, q_gain, k_gain, w_out_hyb,
           w_pw1, b_pw1, w_cdw, b_cdw, ln_g, ln_b, w_pw2, b_pw2, w_up, w_fdw, b_fdw, w_down):
    batch, seq, d = x.shape
    assert batch == 1 and d == D_MODEL
    x2d = x.reshape(seq, d)
    ctx2d = ctx.reshape(-1, d)
    row = lambda v: v.reshape(1, -1)

    cond = jnp.zeros((SUBLANES, d), F32).at[0].set(c[0]).at[1].set(c_ctx)
    mods = _ada(cond, w_ada, b_ada)
    mod = lambda layer, who, j: mods[layer, who:who + 1, j * d:(j + 1) * d]

    w_in = w_in_hyb[0].astype(BF16)
    qg = row(jnp.tile(q_gain[0], LANES // HEAD_DIM))
    kg = row(jnp.tile(k_gain[0], LANES // HEAD_DIM))
    bd = _headnorm_matrix()
    dft = _channel_dft_matrix()
    shift2 = mods[0, 0:2, 0:d].reshape(2, 1, d)
    scale2 = mods[0, 0:2, d:2 * d].reshape(2, 1, d)
    qt, k_h, vt, ab = _inproj(x2d, ctx2d, row(g_mix[0]), shift2, scale2, w_in, qg, kg,
                              _rope_tables(seq), bd, dft)
    att = _attention(qt, k_h, vt)
    fm = _fourier(ab)
    x1 = _outproj(x2d, att, fm, w_out_hyb[0].astype(BF16), mod(0, 0, 2))
    x2 = _ffn(x1, row(g_ffn[0]), mod(0, 0, 3), mod(0, 0, 4), mod(0, 0, 5),
              w_up[0].astype(BF16), w_fdw[0], row(b_fdw[0]), w_down[0].astype(BF16))

    x3 = _conformer(x2, row(g_mix[1]), mod(1, 0, 0), mod(1, 0, 1), mod(1, 0, 2),
                    w_pw1[0].astype(BF16), row(b_pw1[0]), w_cdw[0], row(b_cdw[0]),
                    row(ln_g[0]), row(ln_b[0]), w_pw2[0].astype(BF16), row(b_pw2[0]))
    x4 = _ffn(x3, row(g_ffn[1]), mod(1, 0, 3), mod(1, 0, 4), mod(1, 0, 5),
              w_up[1].astype(BF16), w_fdw[1], row(b_fdw[1]), w_down[1].astype(BF16))
    return x4.reshape(batch, seq, d)
```

```python
import functools
import math

import numpy as np
import jax
import jax.numpy as jnp
from jax import lax
from jax.experimental import pallas as pl
from jax.experimental.pallas import tpu as pltpu

F32 = jnp.float32
BF16 = jnp.bfloat16

D_MODEL = 1024
GRID_W = 64
HEAD_DIM = 64
N_Q_HEADS = 8
N_KV_HEADS = 2
GQA_GROUP = N_Q_HEADS // N_KV_HEADS
Q_W = N_Q_HEADS * HEAD_DIM
KV_W = N_KV_HEADS * HEAD_DIM
F_GROUPS = 8
F_GROUP_DIM = 64
F_W = F_GROUPS * F_GROUP_DIM
HYB_IN = Q_W + 2 * KV_W + F_W
ROPE_HALF = HEAD_DIM // 2
ROPE_THETA = 10000.0
CONV_WIDTH = 31
FFN_DIM = 2816
NORM_EPS = 1e-6
LN_EPS = 1e-5

LANES = 128
SUBLANES = 8
FFT_N = 128
V_ROWS = HEAD_DIM
VMEM_LIMIT = 56 * 1024 * 1024

Q_SCALE = HEAD_DIM ** -0.5 * math.log2(math.e)


def _cparams(n_axes=1):
    return pltpu.CompilerParams(dimension_semantics=("arbitrary",) * n_axes,
                                vmem_limit_bytes=VMEM_LIMIT)


def _const_spec(shape):
    zeros = (0,) * len(shape)
    return pl.BlockSpec(shape, lambda *_: zeros, pipeline_mode=pl.Buffered(1))


def _rms_mod(x, g, shift, scale):
    ms = jnp.mean(x * x, axis=-1, keepdims=True)
    return (x * lax.rsqrt(ms + NORM_EPS) * g) * (1.0 + scale) + shift


def _silu(x):
    return x * (1.0 / (1.0 + jnp.exp(-x)))


def _ada_kernel(cond_ref, w_ref, b_ref, o_ref):
    cnd = cond_ref[...]
    o_ref[0] = jnp.dot(_silu(cnd), w_ref[0], preferred_element_type=F32,
                       precision=lax.Precision.HIGHEST) + b_ref[0]


def _ada(cond, w_ada, b_ada):
    depth, d, n = w_ada.shape
    tn = 1536
    return pl.pallas_call(
        _ada_kernel,
        grid=(depth, n // tn),
        in_specs=[pl.BlockSpec((SUBLANES, d), lambda i, j: (0, 0)),
                  pl.BlockSpec((1, d, tn), lambda i, j: (i, 0, j)),
                  pl.BlockSpec((1, 1, tn), lambda i, j: (i, 0, j))],
        out_specs=pl.BlockSpec((1, SUBLANES, tn), lambda i, j: (i, 0, j)),
        out_shape=jax.ShapeDtypeStruct((depth, SUBLANES, n), F32),
        compiler_params=_cparams(2),
        name="ada",
    )(cond, w_ada, b_ada.reshape(depth, 1, n))


INPROJ_TM = ATTN_TQ = 256
GRID_ROWS_PER_TILE = INPROJ_TM // GRID_W


def _inproj_kernel(x_ref, ctx_ref, gm_ref, sh_ref, sc_ref, w_ref, qg_ref, kg_ref, rcos_ref, rsin_ref,
                   ccos_ref, csin_ref, bd_ref, dft_ref, qt_ref, k_ref, vt_ref, ab_ref, *, n_lat):
    i = pl.program_id(0)
    is_ctx = i >= n_lat
    xin = jnp.where(is_ctx, ctx_ref[...], x_ref[...])
    h = _rms_mod(xin, gm_ref[...], sh_ref[...], sc_ref[...])
    u = jnp.dot(h.astype(BF16), w_ref[...], preferred_element_type=F32)

    def rope_table(row_ref, col_ref, ctx_value):
        rows = [jnp.tile(row_ref[g * SUBLANES:(g + 1) * SUBLANES, :], (GRID_W // SUBLANES, 1))
                for g in range(GRID_ROWS_PER_TILE)]
        t = jnp.concatenate(rows, axis=0) + jnp.tile(col_ref[...], (GRID_ROWS_PER_TILE, 1))
        return jnp.where(is_ctx, ctx_value, t)

    cos = rope_table(rcos_ref, ccos_ref, 1.0)
    sin = rope_table(rsin_ref, csin_ref, 0.0)
    bd = bd_ref[...]
    lane = lax.broadcasted_iota(jnp.int32, cos.shape, 1)
    first = (lane % ROPE_HALF) < (ROPE_HALF // 2)

    def head_norm_rope(t, gain):
        sq = t * t
        hi = sq.astype(BF16)
        lo = (sq - hi.astype(F32)).astype(BF16)
        ms = (jnp.dot(hi, bd, preferred_element_type=F32)
              + jnp.dot(lo, bd, preferred_element_type=F32))
        tn = t * lax.rsqrt(ms + NORM_EPS) * gain
        partner = jnp.where(first, pltpu.roll(tn, LANES - ROPE_HALF // 2, 1),
                            pltpu.roll(tn, ROPE_HALF // 2, 1))
        return tn * cos + partner * sin

    kk = head_norm_rope(u[:, Q_W:Q_W + KV_W], kg_ref[...]).astype(BF16)
    vt = u[:, Q_W + KV_W:Q_W + 2 * KV_W].T
    for kv in range(N_KV_HEADS):
        k_ref[kv] = kk[:, kv * HEAD_DIM:(kv + 1) * HEAD_DIM]
        vt_ref[kv] = vt[kv * HEAD_DIM:(kv + 1) * HEAD_DIM, :].astype(BF16)

    @pl.when(i < n_lat)
    def _():
        qg = qg_ref[...]
        for j in range(Q_W // LANES):
            t = head_norm_rope(u[:, j * LANES:(j + 1) * LANES], qg) * Q_SCALE
            tt = t.T.astype(BF16)
            for hh in range(LANES // HEAD_DIM):
                head = j * (LANES // HEAD_DIM) + hh
                kv, g = head // GQA_GROUP, head % GQA_GROUP
                qt_ref[kv, :, g * INPROJ_TM:(g + 1) * INPROJ_TM] = tt[hh * HEAD_DIM:(hh + 1) * HEAD_DIM, :]
        f = u[:, Q_W + 2 * KV_W:].astype(BF16)
        ab_ref[...] = jnp.dot(f, dft_ref[...], preferred_element_type=F32).astype(BF16)


def _inproj(x2d, ctx2d, gm, shift2, scale2, w_in, qg, kg, rope, bd, dft):
    n = x2d.shape[0]
    tm = INPROJ_TM
    n_lat = n // tm
    assert ctx2d.shape[0] == tm and n % tm == 0
    lk = n + tm
    lat = lambda i: jnp.minimum(i, n_lat - 1)
    vec = lambda w: _const_spec((1, w))
    mod = pl.BlockSpec((None, 1, D_MODEL), lambda i: (i // n_lat, 0, 0))
    rtab = pl.BlockSpec((GRID_ROWS_PER_TILE * SUBLANES, LANES), lambda i: (lat(i), 0))
    return pl.pallas_call(
        functools.partial(_inproj_kernel, n_lat=n_lat),
        grid=(n_lat + 1,),
        in_specs=[pl.BlockSpec((tm, D_MODEL), lambda i: (lat(i), 0)), _const_spec((tm, D_MODEL)),
                  vec(D_MODEL), mod, mod, _const_spec((D_MODEL, HYB_IN)), vec(LANES), vec(LANES),
                  rtab, rtab, _const_spec((GRID_W, LANES)), _const_spec((GRID_W, LANES)),
                  _const_spec((LANES, LANES)), _const_spec((F_W, 2 * F_W))],
        out_specs=[pl.BlockSpec((N_KV_HEADS, None, HEAD_DIM, GQA_GROUP * tm), lambda i: (0, lat(i), 0, 0)),
                   pl.BlockSpec((N_KV_HEADS, tm, HEAD_DIM), lambda i: (0, i, 0)),
                   pl.BlockSpec((N_KV_HEADS, V_ROWS, tm), lambda i: (0, 0, i)),
                   pl.BlockSpec((tm, 2 * F_W), lambda i: (lat(i), 0))],
        out_shape=[jax.ShapeDtypeStruct((N_KV_HEADS, n_lat, HEAD_DIM, GQA_GROUP * tm), BF16),
                   jax.ShapeDtypeStruct((N_KV_HEADS, lk, HEAD_DIM), BF16),
                   jax.ShapeDtypeStruct((N_KV_HEADS, V_ROWS, lk), BF16),
                   jax.ShapeDtypeStruct((n, 2 * F_W), BF16)],
        compiler_params=_cparams(1),
        name="inproj",
    )(x2d, ctx2d, gm, shift2, scale2, w_in, qg, kg, *rope, bd, dft)


def _rope_tables(n):
    inv_freq = 1.0 / (ROPE_THETA ** (np.arange(0, ROPE_HALF, 2, dtype=np.float64) / ROPE_HALF))
    rows = n // GRID_W
    e = np.arange(ROPE_HALF)
    sign = np.where(e < ROPE_HALF // 2, -1.0, 1.0)
    ang_r = np.arange(rows)[:, None] * inv_freq[e % (ROPE_HALF // 2)][None, :]
    ang_c = np.arange(GRID_W)[:, None] * inv_freq[e % (ROPE_HALF // 2)][None, :]

    def two_heads(row_half, col_half):
        t = np.concatenate([row_half, col_half], axis=1)
        return np.concatenate([t, t], axis=1)

    zr, zc = np.zeros_like(ang_r), np.zeros_like(ang_c)
    rep = lambda t: jnp.asarray(np.repeat(t, SUBLANES, axis=0), F32)
    return (rep(two_heads(np.cos(ang_r), zr)), rep(two_heads(np.sin(ang_r) * sign, zr)),
            jnp.asarray(two_heads(zc, np.cos(ang_c)), F32),
            jnp.asarray(two_heads(zc, np.sin(ang_c) * sign), F32))


def _headnorm_matrix():
    i = np.arange(LANES)
    return jnp.asarray((i[:, None] // HEAD_DIM == i[None, :] // HEAD_DIM) / HEAD_DIM, BF16)


def _channel_dft_matrix():
    i = np.arange(F_W)
    same = (i[:, None] // F_GROUP_DIM == i[None, :] // F_GROUP_DIM)
    ang = 2.0 * np.pi * ((i[:, None] % F_GROUP_DIM) * (i[None, :] % F_GROUP_DIM) % F_GROUP_DIM) / F_GROUP_DIM
    s = F_GROUP_DIM ** -0.5
    return jnp.asarray(np.concatenate([np.cos(ang) * same * s, np.sin(ang) * same * s], axis=1), BF16)


ATTN_TK = 256
ATTN_UNROLL = 13
ATTN_SLOT = tuple([0, 1, 2] * (ATTN_UNROLL // 3) + [3])
ATTN_NSLOT = max(ATTN_SLOT) + 1
ATTN_UNSHIFTED_LIMIT = 80.0
ATTN_M_INIT = -1e30


def _attn_kernel(qt_ref, k_ref, vt_ref, o_ref, *scr, tk, unroll, running_max):
    s_scr = scr[:ATTN_NSLOT]
    p_scr = scr[ATTN_NSLOT:2 * ATTN_NSLOT]
    acc_scr = scr[2 * ATTN_NSLOT]
    l_scr = scr[2 * ATTN_NSLOT + 1]
    if running_max:
        m_scr = scr[2 * ATTN_NSLOT + 2]
        a_scr = scr[2 * ATTN_NSLOT + 3:]
    slot_of = lambda s: ATTN_SLOT[s % unroll]
    nb = k_ref.shape[0] // tk
    qt = qt_ref[...]

    def scores(blk, slot):
        off = pl.multiple_of(blk * tk, tk)
        s_scr[slot][...] = jnp.dot(k_ref[pl.ds(off, tk), :], qt, preferred_element_type=F32)

    def probs(slot):
        s = s_scr[slot][...]
        if running_max:
            m_old = m_scr[...]
            m_new = jnp.maximum(m_old, jnp.max(s, axis=0, keepdims=True))
            alpha = jnp.exp2(m_old - m_new)
            a_scr[slot][...] = alpha
            m_scr[...] = m_new
            s = s - m_new
        p = jnp.exp2(s)
        part = p.reshape(tk // SUBLANES, SUBLANES, p.shape[1]).sum(axis=0)
        l_scr[...] = (l_scr[...] * alpha if running_max else l_scr[...]) + part
        p_scr[slot][...] = p.astype(BF16)

    def accumulate(blk, slot):
        off = pl.multiple_of(blk * tk, tk)
        d = jnp.dot(vt_ref[:, pl.ds(off, tk)], p_scr[slot][...], preferred_element_type=F32)
        if running_max:
            acc_scr[...] = acc_scr[...] * a_scr[slot][...] + d
        else:
            acc_scr[...] += d

    def steps(j, n_probs, n_scores):
        for s in range(unroll):
            accumulate(j * unroll + s, slot_of(s))
            if s < n_probs:
                probs(slot_of(s + 1))
            if s < n_scores:
                scores(j * unroll + s + 2, slot_of(s + 2))

    def body(j, carry):
        steps(j, unroll, unroll)
        return carry

    acc_scr[...] = jnp.zeros_like(acc_scr)
    l_scr[...] = jnp.zeros_like(l_scr)
    if running_max:
        m_scr[...] = jnp.full_like(m_scr, ATTN_M_INIT)
    scores(0, slot_of(0))
    scores(1, slot_of(1))
    probs(slot_of(0))
    n_iter = nb // unroll
    lax.fori_loop(0, n_iter - 1, body, 0)
    steps(n_iter - 1, unroll - 1, unroll - 2)
    o = acc_scr[...] * (1.0 / jnp.sum(l_scr[...], axis=0, keepdims=True))
    tq = o.shape[1] // GQA_GROUP
    outs = [o[:, g * tq:(g + 1) * tq].T for g in range(GQA_GROUP)]
    o_ref[...] = jnp.concatenate(outs, axis=1).astype(o_ref.dtype)


def _attention_call(qt, k, vt, *, running_max):
    tq, tk, unroll = ATTN_TQ, ATTN_TK, ATTN_UNROLL
    nq = qt.shape[1]
    lk = k.shape[1]
    assert lk % (tk * unroll) == 0 and lk // tk >= 2 * unroll
    gw = GQA_GROUP * HEAD_DIM
    nlane = GQA_GROUP * tq
    scratch = ([pltpu.VMEM((tk, nlane), F32)] * ATTN_NSLOT
               + [pltpu.VMEM((tk, nlane), BF16)] * ATTN_NSLOT
               + [pltpu.VMEM((V_ROWS, nlane), F32), pltpu.VMEM((SUBLANES, nlane), F32)])
    if running_max:
        scratch += [pltpu.VMEM((1, nlane), F32)] * (1 + ATTN_NSLOT)
    return pl.pallas_call(
        functools.partial(_attn_kernel, tk=tk, unroll=unroll, running_max=running_max),
        grid=(N_KV_HEADS, nq),
        in_specs=[pl.BlockSpec((None, None, HEAD_DIM, nlane), lambda h, i: (h, i, 0, 0)),
                  pl.BlockSpec((None, lk, HEAD_DIM), lambda h, i: (h, 0, 0)),
                  pl.BlockSpec((None, V_ROWS, lk), lambda h, i: (h, 0, 0))],
        out_specs=pl.BlockSpec((tq, gw), lambda h, i: (i, h)),
        out_shape=jax.ShapeDtypeStruct((nq * tq, Q_W), BF16),
        scratch_shapes=scratch,
        compiler_params=_cparams(2),
        name="attention_running_max" if running_max else "attention",
    )(qt, k, vt)


def _attention(qt, k, vt, score_bound):
    return lax.cond(score_bound < ATTN_UNSHIFTED_LIMIT,
                    functools.partial(_attention_call, running_max=False),
                    functools.partial(_attention_call, running_max=True), qt, k, vt)


def _fft1_kernel(ab_ref, ca_ref, cb_ref, tr_ref, ti_ref, *, nblk):
    ca = ca_ref[...]
    cb = cb_ref[...]
    for j in range(nblk):
        a = ab_ref[:, j * 2 * F_W:j * 2 * F_W + F_W]
        b = ab_ref[:, j * 2 * F_W + F_W:(j + 1) * 2 * F_W]
        t = (jnp.dot(ca, a, preferred_element_type=F32)
             + jnp.dot(cb, b, preferred_element_type=F32))
        tr_ref[:, j * F_W:(j + 1) * F_W] = t[:FFT_N].astype(BF16)
        ti_ref[:, j * F_W:(j + 1) * F_W] = t[FFT_N:].astype(BF16)


def _fft2_kernel(tr_ref, ti_ref, m_ref, y_ref, *, nblk):
    for j in range(nblk):
        m = m_ref[j]
        y = (jnp.dot(m[:, :FFT_N], tr_ref[j * FFT_N:(j + 1) * FFT_N, :], preferred_element_type=F32)
             + jnp.dot(m[:, FFT_N:], ti_ref[j * FFT_N:(j + 1) * FFT_N, :], preferred_element_type=F32))
        y_ref[:, j * F_W:(j + 1) * F_W] = y.astype(BF16)


def _fft_tables():
    n = FFT_N
    l = n * n
    s = n ** -0.5
    k = np.arange(n)
    ang1 = 2.0 * np.pi * ((k[:, None] * k[None, :]) % n) / n
    c1, s1 = np.cos(ang1) * s, np.sin(ang1) * s
    ca = np.concatenate([c1, -s1], axis=0)
    cb = np.concatenate([-s1, -c1], axis=0)
    k1 = k[:, None, None]
    k2 = k[None, :, None]
    n2 = k[None, None, :]
    ang2 = 2.0 * np.pi * ((n2 * (k1 + n * k2)) % l) / l
    m = np.concatenate([np.cos(ang2) * s, np.sin(ang2) * s], axis=2)
    return jnp.asarray(ca, BF16), jnp.asarray(cb, BF16), jnp.asarray(m, BF16)


def _fourier(ab):
    l = ab.shape[0]
    assert l == FFT_N * FFT_N
    ca, cb, m = _fft_tables()
    nblk = 16
    ab_v = ab.reshape(FFT_N, FFT_N * 2 * F_W)
    tr, ti = pl.pallas_call(
        functools.partial(_fft1_kernel, nblk=nblk),
        grid=(FFT_N // nblk,),
        in_specs=[pl.BlockSpec((FFT_N, nblk * 2 * F_W), lambda i: (0, i)),
                  _const_spec((2 * FFT_N, FFT_N)), _const_spec((2 * FFT_N, FFT_N))],
        out_specs=[pl.BlockSpec((FFT_N, nblk * F_W), lambda i: (0, i))] * 2,
        out_shape=[jax.ShapeDtypeStruct((FFT_N, FFT_N * F_W), BF16)] * 2,
        compiler_params=_cparams(1),
        name="fft_stage1",
    )(ab_v, ca, cb)
    tr = tr.reshape(l, F_W)
    ti = ti.reshape(l, F_W)
    y = pl.pallas_call(
        functools.partial(_fft2_kernel, nblk=nblk),
        grid=(FFT_N // nblk,),
        in_specs=[pl.BlockSpec((nblk * FFT_N, F_W), lambda i: (i, 0)),
                  pl.BlockSpec((nblk * FFT_N, F_W), lambda i: (i, 0)),
                  pl.BlockSpec((nblk, FFT_N, 2 * FFT_N), lambda i: (i, 0, 0))],
        out_specs=pl.BlockSpec((FFT_N, nblk * F_W), lambda i: (0, i)),
        out_shape=jax.ShapeDtypeStruct((FFT_N, FFT_N * F_W), BF16),
        compiler_params=_cparams(1),
        name="fft_stage2",
    )(tr, ti, m)
    return y.reshape(l, F_W)


FFN_TM = 512
FFN_HALO = SUBLANES
FFN_CHUNKS = 1
BF16_ROWS = 2 * SUBLANES


def _ffn_tile(xp, x, xn, g_ref, sh_ref, sc_ref, gate_ref, wup_ref, wdw_ref, bdw_ref, wdn_ref):
    i = pl.program_id(0)
    last = pl.num_programs(0) - 1
    tm = x.shape[0]
    g, sh, sc = g_ref[...], sh_ref[...], sc_ref[...]
    hp = _rms_mod(xp, g, sh, sc) * jnp.where(i > 0, 1.0, 0.0)
    hn = _rms_mod(xn, g, sh, sc) * jnp.where(i < last, 1.0, 0.0)
    h = jnp.concatenate([hp, _rms_mod(x, g, sh, sc), hn], axis=0).astype(BF16)
    rows = tm + 2 * FFN_HALO
    cw = FFN_DIM // FFN_CHUNKS

    def conv(u, c0):
        w = wdw_ref[:, c0:c0 + cw]
        um = pltpu.roll(u, 1, 0)[FFN_HALO:FFN_HALO + tm]
        up = pltpu.roll(u, rows - 1, 0)[FFN_HALO:FFN_HALO + tm]
        return (um * w[0:1] + u[FFN_HALO:FFN_HALO + tm] * w[1:2] + up * w[2:3]
                + bdw_ref[:, c0:c0 + cw])

    acc = None
    for c in range(FFN_CHUNKS):
        ca, cb = c * cw, FFN_DIM + c * cw
        a = conv(jnp.dot(h, wup_ref[:, ca:ca + cw], preferred_element_type=F32), ca)
        b = conv(jnp.dot(h, wup_ref[:, cb:cb + cw], preferred_element_type=F32), cb)
        act = (_silu(a) * b).astype(BF16)
        d = jnp.dot(act, wdn_ref[ca:ca + cw, :], preferred_element_type=F32)
        acc = d if acc is None else acc + d
    return x + gate_ref[...] * acc


def _ffn_kernel(xp_ref, x_ref, xn_ref, *rest):
    *ffn_refs, o_ref = rest
    o_ref[...] = _ffn_tile(xp_ref[...], x_ref[...], xn_ref[...], *ffn_refs)


def _mix_ffn_kernel(xp_ref, x_ref, xn_ref, ap_ref, a_ref, an_ref, yp_ref, y_ref, yn_ref, wo_ref, g1_ref,
                    *rest):
    *ffn_refs, o_ref = rest
    tm = x_ref.shape[0]
    a_all = jnp.concatenate([a_ref[...], ap_ref[...], an_ref[...]], axis=0)
    y_all = jnp.concatenate([y_ref[...], yp_ref[...], yn_ref[...]], axis=0)
    o_all = (jnp.dot(a_all, wo_ref[:Q_W, :], preferred_element_type=F32)
             + jnp.dot(y_all, wo_ref[Q_W:, :], preferred_element_type=F32))
    g1 = g1_ref[...]
    prev_lo = tm + BF16_ROWS - FFN_HALO
    next_lo = tm + BF16_ROWS
    x1 = x_ref[...] + g1 * o_all[:tm]
    x1p = xp_ref[...] + g1 * o_all[prev_lo:prev_lo + FFN_HALO]
    x1n = xn_ref[...] + g1 * o_all[next_lo:next_lo + FFN_HALO]
    o_ref[...] = _ffn_tile(x1p, x1, x1n, *ffn_refs)


def _halo_specs(n, tm, halo, width):
    hb, nh = tm // halo, n // halo
    return [pl.BlockSpec((halo, width), lambda i: (jnp.maximum(i * hb - 1, 0), 0)),
            pl.BlockSpec((tm, width), lambda i: (i, 0)),
            pl.BlockSpec((halo, width), lambda i: (jnp.minimum((i + 1) * hb, nh - 1), 0))]


def _ffn_weight_specs(layer):
    vec = lambda: _const_spec((1, D_MODEL))
    per_layer = lambda r, c: pl.BlockSpec((None, r, c), lambda i: (layer, 0, 0),
                                          pipeline_mode=pl.Buffered(1))
    return [vec(), vec(), vec(), vec(),
            per_layer(D_MODEL, 2 * FFN_DIM), per_layer(3, 2 * FFN_DIM),
            per_layer(1, 2 * FFN_DIM), per_layer(FFN_DIM, D_MODEL)]


def _ffn(x2d, g, shift, scale, gate, layer, w_up, w_dw, b_dw, w_down):
    n = x2d.shape[0]
    tm = FFN_TM
    return pl.pallas_call(
        _ffn_kernel,
        grid=(n // tm,),
        in_specs=_halo_specs(n, tm, FFN_HALO, D_MODEL) + _ffn_weight_specs(layer),
        out_specs=pl.BlockSpec((tm, D_MODEL), lambda i: (i, 0)),
        out_shape=jax.ShapeDtypeStruct((n, D_MODEL), F32),
        compiler_params=_cparams(1),
        name="ffn",
    )(x2d, x2d, x2d, g, shift, scale, gate, w_up, w_dw, b_dw, w_down)


def _mix_ffn(x2d, att, fm, w_out, gate1, g, shift, scale, gate, layer, w_up, w_dw, b_dw, w_down):
    n = x2d.shape[0]
    tm = FFN_TM
    return pl.pallas_call(
        _mix_ffn_kernel,
        grid=(n // tm,),
        in_specs=(_halo_specs(n, tm, FFN_HALO, D_MODEL) + _halo_specs(n, tm, BF16_ROWS, Q_W)
                  + _halo_specs(n, tm, BF16_ROWS, F_W)
                  + [_const_spec((Q_W + F_W, D_MODEL)), _const_spec((1, D_MODEL))]
                  + _ffn_weight_specs(layer)),
        out_specs=pl.BlockSpec((tm, D_MODEL), lambda i: (i, 0)),
        out_shape=jax.ShapeDtypeStruct((n, D_MODEL), F32),
        compiler_params=_cparams(1),
        name="mix_ffn",
    )(x2d, x2d, x2d, att, att, att, fm, fm, fm, w_out, gate1, g, shift, scale, gate,
      w_up, w_dw, b_dw, w_down)


CONF_HALO = 2 * SUBLANES
CONF_TM = 512
CONF_N = CONF_TM + 2 * CONF_HALO
CONF_NF = 288


def _conv_dft_tables():
    n, nf = CONF_N, CONF_N // 2 + 1
    f = np.arange(CONF_NF)[:, None]
    live = f < nf
    t = np.arange(n)[None, :]
    ang = 2.0 * np.pi * ((f * t) % n) / n
    fwd = np.concatenate([np.cos(ang) * live, np.sin(ang) * live], axis=0)
    r = np.arange(CONF_HALO, CONF_HALO + CONF_TM)[:, None]
    fi = f.T
    weight = np.where((fi == 0) | (fi == n // 2), 1.0, 2.0) * live.T / n
    angi = 2.0 * np.pi * ((r * fi) % n) / n
    inv = np.concatenate([weight * np.cos(angi), -weight * np.sin(angi)], axis=1)
    k = np.arange(CONV_WIDTH)[None, :] - (CONV_WIDTH - 1) // 2
    angw = 2.0 * np.pi * ((f * k) % n) / n
    gw = np.concatenate([np.cos(angw) * live, np.sin(angw) * live], axis=0)
    return jnp.asarray(fwd, BF16), jnp.asarray(inv, BF16), jnp.asarray(gw, F32)


def _filter_spectrum_kernel(gw_ref, w_ref, o_ref):
    o_ref[...] = jnp.dot(gw_ref[...], w_ref[...], preferred_element_type=F32,
                         precision=lax.Precision.HIGHEST)


def _filter_spectrum(gw, wdw):
    return pl.pallas_call(
        _filter_spectrum_kernel,
        out_shape=jax.ShapeDtypeStruct((2 * CONF_NF, D_MODEL), F32),
        name="conv_filter_spectrum",
    )(gw, wdw)


def _conf_kernel(xp_ref, x_ref, xn_ref, g_ref, sh_ref, sc_ref, gate_ref, w1_ref, b1_ref, fwd_ref, inv_ref,
                 gs_ref, bdw_ref, lng_ref, lnb_ref, w2_ref, b2_ref, o_ref, *, seq_len):
    i = pl.program_id(0)
    tm = x_ref.shape[0]
    g, sh, sc = g_ref[...], sh_ref[...], sc_ref[...]
    x = x_ref[...]
    h = jnp.concatenate([_rms_mod(xp_ref[...], g, sh, sc), _rms_mod(x, g, sh, sc),
                         _rms_mod(xn_ref[...], g, sh, sc)], axis=0).astype(BF16)
    u = jnp.dot(h, w1_ref[...], preferred_element_type=F32) + b1_ref[...]
    glu = u[:, :D_MODEL] * (1.0 / (1.0 + jnp.exp(-u[:, D_MODEL:])))
    pos = i * tm - CONF_HALO + lax.broadcasted_iota(jnp.int32, (CONF_N, 1), 0)
    glu = jnp.where((pos >= 0) & (pos < seq_len), glu, 0.0).astype(BF16)
    spec = jnp.dot(fwd_ref[...], glu, preferred_element_type=F32)
    uc, us = spec[:CONF_NF], spec[CONF_NF:]
    gc, gs = gs_ref[:CONF_NF, :], gs_ref[CONF_NF:, :]
    y = jnp.concatenate([uc * gc + us * gs, uc * gs - us * gc], axis=0).astype(BF16)
    acc = jnp.dot(inv_ref[...], y, preferred_element_type=F32) + bdw_ref[...]
    mu = jnp.mean(acc, axis=-1, keepdims=True)
    xc = acc - mu
    var = jnp.mean(xc * xc, axis=-1, keepdims=True)
    yn = xc * lax.rsqrt(var + LN_EPS) * lng_ref[...] + lnb_ref[...]
    o = jnp.dot(_silu(yn).astype(BF16), w2_ref[...], preferred_element_type=F32) + b2_ref[...]
    o_ref[...] = x + gate_ref[...] * o


def _conformer(x2d, g, shift, scale, gate, w1, b1, wdw, bdw, lng, lnb, w2, b2):
    n = x2d.shape[0]
    tm = CONF_TM
    hb = tm // CONF_HALO
    nh = n // CONF_HALO
    fwd, inv, gw = _conv_dft_tables()
    gspec = _filter_spectrum(gw, wdw)
    vec = lambda w=D_MODEL: _const_spec((1, w))
    return pl.pallas_call(
        functools.partial(_conf_kernel, seq_len=n),
        grid=(n // tm,),
        in_specs=[pl.BlockSpec((CONF_HALO, D_MODEL), lambda i: (jnp.maximum(i * hb - 1, 0), 0)),
                  pl.BlockSpec((tm, D_MODEL), lambda i: (i, 0)),
                  pl.BlockSpec((CONF_HALO, D_MODEL), lambda i: (jnp.minimum((i + 1) * hb, nh - 1), 0)),
                  vec(), vec(), vec(), vec(),
                  _const_spec((D_MODEL, 2 * D_MODEL)), vec(2 * D_MODEL),
                  _const_spec((2 * CONF_NF, CONF_N)), _const_spec((tm, 2 * CONF_NF)),
                  _const_spec((2 * CONF_NF, D_MODEL)), vec(), vec(), vec(),
                  _const_spec((D_MODEL, D_MODEL)), vec()],
        out_specs=pl.BlockSpec((tm, D_MODEL), lambda i: (i, 0)),
        out_shape=jax.ShapeDtypeStruct((n, D_MODEL), F32),
        compiler_params=_cparams(1),
        name="conformer",
    )(x2d, x2d, x2d, g, shift, scale, gate, w1, b1, fwd, inv, gspec, bdw, lng, lnb, w2, b2)


def kernel(x, c, ctx, c_ctx, w_ada, b_ada, g_mix, g_ffn, w_in_hyb, q_gain, k_gain, w_out_hyb,
           w_pw1, b_pw1, w_cdw, b_cdw, ln_g, ln_b, w_pw2, b_pw2, w_up, w_fdw, b_fdw, w_down):
    batch, seq, d = x.shape
    assert batch == 1 and d == D_MODEL
    x2d = x.reshape(seq, d)
    ctx2d = ctx.reshape(-1, d)
    row = lambda v: v.reshape(1, -1)

    cond = jnp.zeros((SUBLANES, d), F32).at[0].set(c[0]).at[1].set(c_ctx)
    mods = _ada(cond, w_ada, b_ada)
    mod = lambda layer, who, j: mods[layer, who:who + 1, j * d:(j + 1) * d]

    w_in = w_in_hyb[0].astype(BF16)
    qg = row(jnp.tile(q_gain[0], LANES // HEAD_DIM))
    kg = row(jnp.tile(k_gain[0], LANES // HEAD_DIM))
    bd = _headnorm_matrix()
    dft = _channel_dft_matrix()
    shift2 = mods[0, 0:2, 0:d].reshape(2, 1, d)
    scale2 = mods[0, 0:2, d:2 * d].reshape(2, 1, d)
    qt, k_h, vt, ab = _inproj(x2d, ctx2d, row(g_mix[0]), shift2, scale2, w_in, qg, kg,
                              _rope_tables(seq), bd, dft)
    score_bound = HEAD_DIM * Q_SCALE * jnp.max(jnp.abs(q_gain[0])) * jnp.max(jnp.abs(k_gain[0]))
    att = _attention(qt, k_h, vt, score_bound)
    fm = _fourier(ab)
    ffn_w = (w_up.astype(BF16), w_fdw, b_fdw.reshape(b_fdw.shape[0], 1, -1), w_down.astype(BF16))
    x2 = _mix_ffn(x2d, att, fm, w_out_hyb[0].astype(BF16), mod(0, 0, 2),
                  row(g_ffn[0]), mod(0, 0, 3), mod(0, 0, 4), mod(0, 0, 5), 0, *ffn_w)

    x3 = _conformer(x2, row(g_mix[1]), mod(1, 0, 0), mod(1, 0, 1), mod(1, 0, 2),
                    w_pw1[0].astype(BF16), row(b_pw1[0]), w_cdw[0], row(b_cdw[0]),
                    row(ln_g[0]), row(ln_b[0]), w_pw2[0].astype(BF16), row(b_pw2[0]))
    x4 = _ffn(x3, row(g_ffn[1]), mod(1, 0, 3), mod(1, 0, 4), mod(1, 0, 5), 1, *ffn_w)
    return x4.reshape(batch, seq, d)
```

```python
import functools
import math

import numpy as np
import jax
import jax.numpy as jnp
from jax import lax
from jax.experimental import pallas as pl
from jax.experimental.pallas import tpu as pltpu

F32 = jnp.float32
BF16 = jnp.bfloat16

D_MODEL = 1024
GRID_W = 64
HEAD_DIM = 64
N_Q_HEADS = 8
N_KV_HEADS = 2
GQA_GROUP = N_Q_HEADS // N_KV_HEADS
Q_W = N_Q_HEADS * HEAD_DIM
KV_W = N_KV_HEADS * HEAD_DIM
F_GROUPS = 8
F_GROUP_DIM = 64
F_W = F_GROUPS * F_GROUP_DIM
HYB_IN = Q_W + 2 * KV_W + F_W
ROPE_HALF = HEAD_DIM // 2
ROPE_THETA = 10000.0
CONV_WIDTH = 31
FFN_DIM = 2816
NORM_EPS = 1e-6
LN_EPS = 1e-5

LANES = 128
SUBLANES = 8
FFT_N = 128
V_ROWS = 80
VMEM_LIMIT = 56 * 1024 * 1024

Q_SCALE = HEAD_DIM ** -0.5 * math.log2(math.e)


def _cparams(n_axes=1):
    return pltpu.CompilerParams(dimension_semantics=("arbitrary",) * n_axes,
                                vmem_limit_bytes=VMEM_LIMIT)


def _const_spec(shape):
    zeros = (0,) * len(shape)
    return pl.BlockSpec(shape, lambda *_: zeros, pipeline_mode=pl.Buffered(1))


def _rms_mod(x, g, shift, scale):
    ms = jnp.mean(x * x, axis=-1, keepdims=True)
    return (x * lax.rsqrt(ms + NORM_EPS) * g) * (1.0 + scale) + shift


def _silu(x):
    return x * (1.0 / (1.0 + jnp.exp(-x)))


def _ada_kernel(cond_ref, w_ref, b_ref, o_ref):
    cnd = cond_ref[...]
    o_ref[0] = jnp.dot(_silu(cnd), w_ref[0], preferred_element_type=F32,
                       precision=lax.Precision.HIGHEST) + b_ref[0]


def _ada(cond, w_ada, b_ada):
    depth, d, n = w_ada.shape
    tn = 1536
    return pl.pallas_call(
        _ada_kernel,
        grid=(depth, n // tn),
        in_specs=[pl.BlockSpec((SUBLANES, d), lambda i, j: (0, 0)),
                  pl.BlockSpec((1, d, tn), lambda i, j: (i, 0, j)),
                  pl.BlockSpec((1, 1, tn), lambda i, j: (i, 0, j))],
        out_specs=pl.BlockSpec((1, SUBLANES, tn), lambda i, j: (i, 0, j)),
        out_shape=jax.ShapeDtypeStruct((depth, SUBLANES, n), F32),
        compiler_params=_cparams(2),
        name="ada",
    )(cond, w_ada, b_ada.reshape(depth, 1, n))


INPROJ_TM = ATTN_TQ = 256
GRID_ROWS_PER_TILE = INPROJ_TM // GRID_W


def _inproj_kernel(x_ref, ctx_ref, gm_ref, sh_ref, sc_ref, w_ref, qg_ref, kg_ref, rcos_ref, rsin_ref,
                   ccos_ref, csin_ref, bd_ref, dft_ref, qt_ref, k_ref, vt_ref, ab_ref, *, n_lat):
    i = pl.program_id(0)
    is_ctx = i >= n_lat
    xin = jnp.where(is_ctx, ctx_ref[...], x_ref[...])
    h = _rms_mod(xin, gm_ref[...], sh_ref[...], sc_ref[...])
    u = jnp.dot(h.astype(BF16), w_ref[...], preferred_element_type=F32)

    def rope_table(row_ref, col_ref, ctx_value):
        rows = [jnp.tile(row_ref[g * SUBLANES:(g + 1) * SUBLANES, :], (GRID_W // SUBLANES, 1))
                for g in range(GRID_ROWS_PER_TILE)]
        t = jnp.concatenate(rows, axis=0) + jnp.tile(col_ref[...], (GRID_ROWS_PER_TILE, 1))
        return jnp.where(is_ctx, ctx_value, t)

    cos = rope_table(rcos_ref, ccos_ref, 1.0)
    sin = rope_table(rsin_ref, csin_ref, 0.0)
    bd = bd_ref[...]
    lane = lax.broadcasted_iota(jnp.int32, cos.shape, 1)
    first = (lane % ROPE_HALF) < (ROPE_HALF // 2)

    def head_norm_rope(t, gain):
        sq = t * t
        hi = sq.astype(BF16)
        lo = (sq - hi.astype(F32)).astype(BF16)
        ms = (jnp.dot(hi, bd, preferred_element_type=F32)
              + jnp.dot(lo, bd, preferred_element_type=F32))
        tn = t * lax.rsqrt(ms + NORM_EPS) * gain
        partner = jnp.where(first, pltpu.roll(tn, LANES - ROPE_HALF // 2, 1),
                            pltpu.roll(tn, ROPE_HALF // 2, 1))
        return tn * cos + partner * sin

    kk = head_norm_rope(u[:, Q_W:Q_W + KV_W], kg_ref[...]).astype(BF16)
    vt = u[:, Q_W + KV_W:Q_W + 2 * KV_W].T
    pad_rows = lax.broadcasted_iota(jnp.int32, (V_ROWS - HEAD_DIM, INPROJ_TM), 0)
    ones_rows = jnp.where(pad_rows == 0, 1.0, 0.0).astype(BF16)
    for kv in range(N_KV_HEADS):
        k_ref[kv] = kk[:, kv * HEAD_DIM:(kv + 1) * HEAD_DIM]
        vt_ref[kv, :HEAD_DIM, :] = vt[kv * HEAD_DIM:(kv + 1) * HEAD_DIM, :].astype(BF16)
        vt_ref[kv, HEAD_DIM:, :] = ones_rows

    @pl.when(i < n_lat)
    def _():
        qg = qg_ref[...]
        for j in range(Q_W // LANES):
            t = head_norm_rope(u[:, j * LANES:(j + 1) * LANES], qg) * Q_SCALE
            tt = t.T.astype(BF16)
            for hh in range(LANES // HEAD_DIM):
                head = j * (LANES // HEAD_DIM) + hh
                kv, g = head // GQA_GROUP, head % GQA_GROUP
                qt_ref[kv, :, g * INPROJ_TM:(g + 1) * INPROJ_TM] = tt[hh * HEAD_DIM:(hh + 1) * HEAD_DIM, :]
        f = u[:, Q_W + 2 * KV_W:].astype(BF16)
        ab_ref[...] = jnp.dot(f, dft_ref[...], preferred_element_type=F32).astype(BF16)


def _inproj(x2d, ctx2d, gm, shift2, scale2, w_in, qg, kg, rope, bd, dft):
    n = x2d.shape[0]
    tm = INPROJ_TM
    n_lat = n // tm
    assert ctx2d.shape[0] == tm and n % tm == 0
    lk = n + tm
    lat = lambda i: jnp.minimum(i, n_lat - 1)
    vec = lambda w: _const_spec((1, w))
    mod = pl.BlockSpec((None, 1, D_MODEL), lambda i: (i // n_lat, 0, 0))
    rtab = pl.BlockSpec((GRID_ROWS_PER_TILE * SUBLANES, LANES), lambda i: (lat(i), 0))
    return pl.pallas_call(
        functools.partial(_inproj_kernel, n_lat=n_lat),
        grid=(n_lat + 1,),
        in_specs=[pl.BlockSpec((tm, D_MODEL), lambda i: (lat(i), 0)), _const_spec((tm, D_MODEL)),
                  vec(D_MODEL), mod, mod, _const_spec((D_MODEL, HYB_IN)), vec(LANES), vec(LANES),
                  rtab, rtab, _const_spec((GRID_W, LANES)), _const_spec((GRID_W, LANES)),
                  _const_spec((LANES, LANES)), _const_spec((F_W, 2 * F_W))],
        out_specs=[pl.BlockSpec((N_KV_HEADS, None, HEAD_DIM, GQA_GROUP * tm), lambda i: (0, lat(i), 0, 0)),
                   pl.BlockSpec((N_KV_HEADS, tm, HEAD_DIM), lambda i: (0, i, 0)),
                   pl.BlockSpec((N_KV_HEADS, V_ROWS, tm), lambda i: (0, 0, i)),
                   pl.BlockSpec((tm, 2 * F_W), lambda i: (lat(i), 0))],
        out_shape=[jax.ShapeDtypeStruct((N_KV_HEADS, n_lat, HEAD_DIM, GQA_GROUP * tm), BF16),
                   jax.ShapeDtypeStruct((N_KV_HEADS, lk, HEAD_DIM), BF16),
                   jax.ShapeDtypeStruct((N_KV_HEADS, V_ROWS, lk), BF16),
                   jax.ShapeDtypeStruct((n, 2 * F_W), BF16)],
        compiler_params=_cparams(1),
        name="inproj",
    )(x2d, ctx2d, gm, shift2, scale2, w_in, qg, kg, *rope, bd, dft)


def _rope_tables(n):
    inv_freq = 1.0 / (ROPE_THETA ** (np.arange(0, ROPE_HALF, 2, dtype=np.float64) / ROPE_HALF))
    rows = n // GRID_W
    e = np.arange(ROPE_HALF)
    sign = np.where(e < ROPE_HALF // 2, -1.0, 1.0)
    ang_r = np.arange(rows)[:, None] * inv_freq[e % (ROPE_HALF // 2)][None, :]
    ang_c = np.arange(GRID_W)[:, None] * inv_freq[e % (ROPE_HALF // 2)][None, :]

    def two_heads(row_half, col_half):
        t = np.concatenate([row_half, col_half], axis=1)
        return np.concatenate([t, t], axis=1)

    zr, zc = np.zeros_like(ang_r), np.zeros_like(ang_c)
    rep = lambda t: jnp.asarray(np.repeat(t, SUBLANES, axis=0), F32)
    return (rep(two_heads(np.cos(ang_r), zr)), rep(two_heads(np.sin(ang_r) * sign, zr)),
            jnp.asarray(two_heads(zc, np.cos(ang_c)), F32),
            jnp.asarray(two_heads(zc, np.sin(ang_c) * sign), F32))


def _headnorm_matrix():
    i = np.arange(LANES)
    return jnp.asarray((i[:, None] // HEAD_DIM == i[None, :] // HEAD_DIM) / HEAD_DIM, BF16)


def _channel_dft_matrix():
    i = np.arange(F_W)
    same = (i[:, None] // F_GROUP_DIM == i[None, :] // F_GROUP_DIM)
    ang = 2.0 * np.pi * ((i[:, None] % F_GROUP_DIM) * (i[None, :] % F_GROUP_DIM) % F_GROUP_DIM) / F_GROUP_DIM
    s = F_GROUP_DIM ** -0.5
    return jnp.asarray(np.concatenate([np.cos(ang) * same * s, np.sin(ang) * same * s], axis=1), BF16)


ATTN_TK = 256
ATTN_UNROLL = 65
ATTN_SLOT = tuple(s % 6 for s in range(ATTN_UNROLL))
ATTN_NSLOT = max(ATTN_SLOT) + 1
ATTN_UNSHIFTED_LIMIT = 80.0
ATTN_M_INIT = -1e30


def _attn_kernel(qt_ref, k_ref, vt_ref, o_ref, *scr, tk, unroll, running_max):
    s_scr = scr[:ATTN_NSLOT]
    p_scr = scr[ATTN_NSLOT:2 * ATTN_NSLOT]
    acc_scr = scr[2 * ATTN_NSLOT]
    if running_max:
        m_scr = scr[2 * ATTN_NSLOT + 1]
        a_scr = scr[2 * ATTN_NSLOT + 2:]
    slot_of = lambda s: ATTN_SLOT[s % unroll]
    nb = k_ref.shape[0] // tk
    qt = qt_ref[...]

    def scores(blk, slot):
        off = pl.multiple_of(blk * tk, tk)
        s_scr[slot][...] = jnp.dot(k_ref[pl.ds(off, tk), :], qt, preferred_element_type=F32)

    def probs(slot):
        s = s_scr[slot][...]
        if running_max:
            m_old = m_scr[...]
            m_new = jnp.maximum(m_old, jnp.max(s, axis=0, keepdims=True))
            a_scr[slot][...] = jnp.exp2(m_old - m_new)
            m_scr[...] = m_new
            s = s - m_new
        p_scr[slot][...] = jnp.exp2(s).astype(BF16)

    def accumulate(blk, slot):
        off = pl.multiple_of(blk * tk, tk)
        d = jnp.dot(vt_ref[:, pl.ds(off, tk)], p_scr[slot][...], preferred_element_type=F32)
        if running_max:
            acc_scr[...] = acc_scr[...] * a_scr[slot][...] + d
        else:
            acc_scr[...] += d

    def steps(j, n_probs, n_scores):
        for s in range(unroll):
            accumulate(j * unroll + s, slot_of(s))
            if s < n_probs:
                probs(slot_of(s + 1))
            if s < n_scores:
                scores(j * unroll + s + 2, slot_of(s + 2))

    def body(j, carry):
        steps(j, unroll, unroll)
        return carry

    acc_scr[...] = jnp.zeros_like(acc_scr)
    if running_max:
        m_scr[...] = jnp.full_like(m_scr, ATTN_M_INIT)
    scores(0, slot_of(0))
    scores(1, slot_of(1))
    probs(slot_of(0))
    n_iter = nb // unroll
    lax.fori_loop(0, n_iter - 1, body, 0)
    steps(n_iter - 1, unroll - 1, unroll - 2)
    acc = acc_scr[...]
    o = acc[:HEAD_DIM, :] * (1.0 / acc[HEAD_DIM:HEAD_DIM + 1, :])
    tq = o.shape[1] // GQA_GROUP
    outs = [o[:, g * tq:(g + 1) * tq].T for g in range(GQA_GROUP)]
    o_ref[...] = jnp.concatenate(outs, axis=1).astype(o_ref.dtype)


def _attention_call(qt, k, vt, *, running_max):
    tq, tk, unroll = ATTN_TQ, ATTN_TK, ATTN_UNROLL
    nq = qt.shape[1]
    lk = k.shape[1]
    assert lk % (tk * unroll) == 0 and unroll >= 3
    gw = GQA_GROUP * HEAD_DIM
    nlane = GQA_GROUP * tq
    scratch = ([pltpu.VMEM((tk, nlane), F32)] * ATTN_NSLOT
               + [pltpu.VMEM((tk, nlane), BF16)] * ATTN_NSLOT
               + [pltpu.VMEM((V_ROWS, nlane), F32)])
    if running_max:
        scratch += [pltpu.VMEM((1, nlane), F32)] * (1 + ATTN_NSLOT)
    return pl.pallas_call(
        functools.partial(_attn_kernel, tk=tk, unroll=unroll, running_max=running_max),
        grid=(N_KV_HEADS, nq),
        in_specs=[pl.BlockSpec((None, None, HEAD_DIM, nlane), lambda h, i: (h, i, 0, 0)),
                  pl.BlockSpec((None, lk, HEAD_DIM), lambda h, i: (h, 0, 0)),
                  pl.BlockSpec((None, V_ROWS, lk), lambda h, i: (h, 0, 0))],
        out_specs=pl.BlockSpec((tq, gw), lambda h, i: (i, h)),
        out_shape=jax.ShapeDtypeStruct((nq * tq, Q_W), BF16),
        scratch_shapes=scratch,
        compiler_params=_cparams(2),
        name="attention_running_max" if running_max else "attention",
    )(qt, k, vt)


def _attention(qt, k, vt, score_bound):
    return lax.cond(score_bound < ATTN_UNSHIFTED_LIMIT,
                    functools.partial(_attention_call, running_max=False),
                    functools.partial(_attention_call, running_max=True), qt, k, vt)


def _fft1_kernel(ab_ref, ca_ref, cb_ref, tr_ref, ti_ref, *, nblk):
    ca = ca_ref[...]
    cb = cb_ref[...]
    for j in range(nblk):
        a = ab_ref[:, j * 2 * F_W:j * 2 * F_W + F_W]
        b = ab_ref[:, j * 2 * F_W + F_W:(j + 1) * 2 * F_W]
        t = (jnp.dot(ca, a, preferred_element_type=F32)
             + jnp.dot(cb, b, preferred_element_type=F32))
        tr_ref[:, j * F_W:(j + 1) * F_W] = t[:FFT_N].astype(BF16)
        ti_ref[:, j * F_W:(j + 1) * F_W] = t[FFT_N:].astype(BF16)


def _fft2_kernel(tr_ref, ti_ref, m_ref, y_ref, *, nblk):
    for j in range(nblk):
        m = m_ref[j]
        y = (jnp.dot(m[:, :FFT_N], tr_ref[j * FFT_N:(j + 1) * FFT_N, :], preferred_element_type=F32)
             + jnp.dot(m[:, FFT_N:], ti_ref[j * FFT_N:(j + 1) * FFT_N, :], preferred_element_type=F32))
        y_ref[:, j * F_W:(j + 1) * F_W] = y.astype(BF16)


def _fft_tables():
    n = FFT_N
    l = n * n
    s = n ** -0.5
    k = np.arange(n)
    ang1 = 2.0 * np.pi * ((k[:, None] * k[None, :]) % n) / n
    c1, s1 = np.cos(ang1) * s, np.sin(ang1) * s
    ca = np.concatenate([c1, -s1], axis=0)
    cb = np.concatenate([-s1, -c1], axis=0)
    k1 = k[:, None, None]
    k2 = k[None, :, None]
    n2 = k[None, None, :]
    ang2 = 2.0 * np.pi * ((n2 * (k1 + n * k2)) % l) / l
    m = np.concatenate([np.cos(ang2) * s, np.sin(ang2) * s], axis=2)
    return jnp.asarray(ca, BF16), jnp.asarray(cb, BF16), jnp.asarray(m, BF16)


def _fourier(ab):
    l = ab.shape[0]
    assert l == FFT_N * FFT_N
    ca, cb, m = _fft_tables()
    nblk = 16
    ab_v = ab.reshape(FFT_N, FFT_N * 2 * F_W)
    tr, ti = pl.pallas_call(
        functools.partial(_fft1_kernel, nblk=nblk),
        grid=(FFT_N // nblk,),
        in_specs=[pl.BlockSpec((FFT_N, nblk * 2 * F_W), lambda i: (0, i)),
                  _const_spec((2 * FFT_N, FFT_N)), _const_spec((2 * FFT_N, FFT_N))],
        out_specs=[pl.BlockSpec((FFT_N, nblk * F_W), lambda i: (0, i))] * 2,
        out_shape=[jax.ShapeDtypeStruct((FFT_N, FFT_N * F_W), BF16)] * 2,
        compiler_params=_cparams(1),
        name="fft_stage1",
    )(ab_v, ca, cb)
    tr = tr.reshape(l, F_W)
    ti = ti.reshape(l, F_W)
    y = pl.pallas_call(
        functools.partial(_fft2_kernel, nblk=nblk),
        grid=(FFT_N // nblk,),
        in_specs=[pl.BlockSpec((nblk * FFT_N, F_W), lambda i: (i, 0)),
                  pl.BlockSpec((nblk * FFT_N, F_W), lambda i: (i, 0)),
                  pl.BlockSpec((nblk, FFT_N, 2 * FFT_N), lambda i: (i, 0, 0))],
        out_specs=pl.BlockSpec((FFT_N, nblk * F_W), lambda i: (0, i)),
        out_shape=jax.ShapeDtypeStruct((FFT_N, FFT_N * F_W), BF16),
        compiler_params=_cparams(1),
        name="fft_stage2",
    )(tr, ti, m)
    return y.reshape(l, F_W)


FFN_TM = 512
FFN_HALO = SUBLANES
FFN_CHUNKS = 1
BF16_ROWS = 2 * SUBLANES


def _ffn_tile(xp, x, xn, g_ref, sh_ref, sc_ref, gate_ref, wup_ref, wdw_ref, bdw_ref, wdn_ref):
    i = pl.program_id(0)
    last = pl.num_programs(0) - 1
    tm = x.shape[0]
    g, sh, sc = g_ref[...], sh_ref[...], sc_ref[...]
    hp = _rms_mod(xp, g, sh, sc) * jnp.where(i > 0, 1.0, 0.0)
    hn = _rms_mod(xn, g, sh, sc) * jnp.where(i < last, 1.0, 0.0)
    h = jnp.concatenate([hp, _rms_mod(x, g, sh, sc), hn], axis=0).astype(BF16)
    rows = tm + 2 * FFN_HALO
    cw = FFN_DIM // FFN_CHUNKS

    def conv(u, c0):
        w = wdw_ref[:, c0:c0 + cw]
        um = pltpu.roll(u, 1, 0)[FFN_HALO:FFN_HALO + tm]
        up = pltpu.roll(u, rows - 1, 0)[FFN_HALO:FFN_HALO + tm]
        return (um * w[0:1] + u[FFN_HALO:FFN_HALO + tm] * w[1:2] + up * w[2:3]
                + bdw_ref[:, c0:c0 + cw])

    acc = None
    for c in range(FFN_CHUNKS):
        ca, cb = c * cw, FFN_DIM + c * cw
        a = conv(jnp.dot(h, wup_ref[:, ca:ca + cw], preferred_element_type=F32), ca)
        b = conv(jnp.dot(h, wup_ref[:, cb:cb + cw], preferred_element_type=F32), cb)
        act = (_silu(a) * b).astype(BF16)
        d = jnp.dot(act, wdn_ref[ca:ca + cw, :], preferred_element_type=F32)
        acc = d if acc is None else acc + d
    return x + gate_ref[...] * acc


def _ffn_kernel(xp_ref, x_ref, xn_ref, *rest):
    *ffn_refs, o_ref = rest
    o_ref[...] = _ffn_tile(xp_ref[...], x_ref[...], xn_ref[...], *ffn_refs)


def _mix_ffn_kernel(xp_ref, x_ref, xn_ref, ap_ref, a_ref, an_ref, yp_ref, y_ref, yn_ref, wo_ref, g1_ref,
                    *rest):
    *ffn_refs, o_ref = rest
    tm = x_ref.shape[0]
    a_all = jnp.concatenate([a_ref[...], ap_ref[...], an_ref[...]], axis=0)
    y_all = jnp.concatenate([y_ref[...], yp_ref[...], yn_ref[...]], axis=0)
    o_all = (jnp.dot(a_all, wo_ref[:Q_W, :], preferred_element_type=F32)
             + jnp.dot(y_all, wo_ref[Q_W:, :], preferred_element_type=F32))
    g1 = g1_ref[...]
    prev_lo = tm + BF16_ROWS - FFN_HALO
    next_lo = tm + BF16_ROWS
    x1 = x_ref[...] + g1 * o_all[:tm]
    x1p = xp_ref[...] + g1 * o_all[prev_lo:prev_lo + FFN_HALO]
    x1n = xn_ref[...] + g1 * o_all[next_lo:next_lo + FFN_HALO]
    o_ref[...] = _ffn_tile(x1p, x1, x1n, *ffn_refs)


def _halo_specs(n, tm, halo, width):
    hb, nh = tm // halo, n // halo
    return [pl.BlockSpec((halo, width), lambda i: (jnp.maximum(i * hb - 1, 0), 0)),
            pl.BlockSpec((tm, width), lambda i: (i, 0)),
            pl.BlockSpec((halo, width), lambda i: (jnp.minimum((i + 1) * hb, nh - 1), 0))]


def _ffn_weight_specs(layer):
    vec = lambda: _const_spec((1, D_MODEL))
    per_layer = lambda r, c: pl.BlockSpec((None, r, c), lambda i: (layer, 0, 0),
                                          pipeline_mode=pl.Buffered(1))
    return [vec(), vec(), vec(), vec(),
            per_layer(D_MODEL, 2 * FFN_DIM), per_layer(3, 2 * FFN_DIM),
            per_layer(1, 2 * FFN_DIM), per_layer(FFN_DIM, D_MODEL)]


def _ffn(x2d, g, shift, scale, gate, layer, w_up, w_dw, b_dw, w_down):
    n = x2d.shape[0]
    tm = FFN_TM
    return pl.pallas_call(
        _ffn_kernel,
        grid=(n // tm,),
        in_specs=_halo_specs(n, tm, FFN_HALO, D_MODEL) + _ffn_weight_specs(layer),
        out_specs=pl.BlockSpec((tm, D_MODEL), lambda i: (i, 0)),
        out_shape=jax.ShapeDtypeStruct((n, D_MODEL), F32),
        compiler_params=_cparams(1),
        name="ffn",
    )(x2d, x2d, x2d, g, shift, scale, gate, w_up, w_dw, b_dw, w_down)


def _mix_ffn(x2d, att, fm, w_out, gate1, g, shift, scale, gate, layer, w_up, w_dw, b_dw, w_down):
    n = x2d.shape[0]
    tm = FFN_TM
    return pl.pallas_call(
        _mix_ffn_kernel,
        grid=(n // tm,),
        in_specs=(_halo_specs(n, tm, FFN_HALO, D_MODEL) + _halo_specs(n, tm, BF16_ROWS, Q_W)
                  + _halo_specs(n, tm, BF16_ROWS, F_W)
                  + [_const_spec((Q_W + F_W, D_MODEL)), _const_spec((1, D_MODEL))]
                  + _ffn_weight_specs(layer)),
        out_specs=pl.BlockSpec((tm, D_MODEL), lambda i: (i, 0)),
        out_shape=jax.ShapeDtypeStruct((n, D_MODEL), F32),
        compiler_params=_cparams(1),
        name="mix_ffn",
    )(x2d, x2d, x2d, att, att, att, fm, fm, fm, w_out, gate1, g, shift, scale, gate,
      w_up, w_dw, b_dw, w_down)


CONF_HALO = 2 * SUBLANES
CONF_TM = 512
CONF_N = CONF_TM + 2 * CONF_HALO
CONF_NF = 288


def _conv_dft_tables():
    n, nf = CONF_N, CONF_N // 2 + 1
    f = np.arange(CONF_NF)[:, None]
    live = f < nf
    t = np.arange(n)[None, :]
    ang = 2.0 * np.pi * ((f * t) % n) / n
    fwd = np.concatenate([np.cos(ang) * live, np.sin(ang) * live], axis=0)
    r = np.arange(CONF_HALO, CONF_HALO + CONF_TM)[:, None]
    fi = f.T
    weight = np.where((fi == 0) | (fi == n // 2), 1.0, 2.0) * live.T / n
    angi = 2.0 * np.pi * ((r * fi) % n) / n
    inv = np.concatenate([weight * np.cos(angi), -weight * np.sin(angi)], axis=1)
    k = np.arange(CONV_WIDTH)[None, :] - (CONV_WIDTH - 1) // 2
    angw = 2.0 * np.pi * ((f * k) % n) / n
    gw = np.concatenate([np.cos(angw) * live, np.sin(angw) * live], axis=0)
    return jnp.asarray(fwd, BF16), jnp.asarray(inv, BF16), jnp.asarray(gw, F32)


def _filter_spectrum_kernel(gw_ref, w_ref, o_ref):
    o_ref[...] = jnp.dot(gw_ref[...], w_ref[...], preferred_element_type=F32,
                         precision=lax.Precision.HIGHEST)


def _filter_spectrum(gw, wdw):
    return pl.pallas_call(
        _filter_spectrum_kernel,
        out_shape=jax.ShapeDtypeStruct((2 * CONF_NF, D_MODEL), F32),
        name="conv_filter_spectrum",
    )(gw, wdw)


def _conf_kernel(xp_ref, x_ref, xn_ref, g_ref, sh_ref, sc_ref, gate_ref, w1_ref, b1_ref, fwd_ref, inv_ref,
                 gs_ref, bdw_ref, lng_ref, lnb_ref, w2_ref, b2_ref, o_ref, *, seq_len):
    i = pl.program_id(0)
    tm = x_ref.shape[0]
    g, sh, sc = g_ref[...], sh_ref[...], sc_ref[...]
    x = x_ref[...]
    h = jnp.concatenate([_rms_mod(xp_ref[...], g, sh, sc), _rms_mod(x, g, sh, sc),
                         _rms_mod(xn_ref[...], g, sh, sc)], axis=0).astype(BF16)
    u = jnp.dot(h, w1_ref[...], preferred_element_type=F32) + b1_ref[...]
    glu = u[:, :D_MODEL] * (1.0 / (1.0 + jnp.exp(-u[:, D_MODEL:])))
    pos = i * tm - CONF_HALO + lax.broadcasted_iota(jnp.int32, (CONF_N, 1), 0)
    glu = jnp.where((pos >= 0) & (pos < seq_len), glu, 0.0).astype(BF16)
    spec = jnp.dot(fwd_ref[...], glu, preferred_element_type=F32)
    uc, us = spec[:CONF_NF], spec[CONF_NF:]
    gc, gs = gs_ref[:CONF_NF, :], gs_ref[CONF_NF:, :]
    y = jnp.concatenate([uc * gc + us * gs, uc * gs - us * gc], axis=0).astype(BF16)
    acc = jnp.dot(inv_ref[...], y, preferred_element_type=F32) + bdw_ref[...]
    mu = jnp.mean(acc, axis=-1, keepdims=True)
    xc = acc - mu
    var = jnp.mean(xc * xc, axis=-1, keepdims=True)
    yn = xc * lax.rsqrt(var + LN_EPS) * lng_ref[...] + lnb_ref[...]
    o = jnp.dot(_silu(yn).astype(BF16), w2_ref[...], preferred_element_type=F32) + b2_ref[...]
    o_ref[...] = x + gate_ref[...] * o


def _conformer(x2d, g, shift, scale, gate, w1, b1, wdw, bdw, lng, lnb, w2, b2):
    n = x2d.shape[0]
    tm = CONF_TM
    hb = tm // CONF_HALO
    nh = n // CONF_HALO
    fwd, inv, gw = _conv_dft_tables()
    gspec = _filter_spectrum(gw, wdw)
    vec = lambda w=D_MODEL: _const_spec((1, w))
    return pl.pallas_call(
        functools.partial(_conf_kernel, seq_len=n),
        grid=(n // tm,),
        in_specs=[pl.BlockSpec((CONF_HALO, D_MODEL), lambda i: (jnp.maximum(i * hb - 1, 0), 0)),
                  pl.BlockSpec((tm, D_MODEL), lambda i: (i, 0)),
                  pl.BlockSpec((CONF_HALO, D_MODEL), lambda i: (jnp.minimum((i + 1) * hb, nh - 1), 0)),
                  vec(), vec(), vec(), vec(),
                  _const_spec((D_MODEL, 2 * D_MODEL)), vec(2 * D_MODEL),
                  _const_spec((2 * CONF_NF, CONF_N)), _const_spec((tm, 2 * CONF_NF)),
                  _const_spec((2 * CONF_NF, D_MODEL)), vec(), vec(), vec(),
                  _const_spec((D_MODEL, D_MODEL)), vec()],
        out_specs=pl.BlockSpec((tm, D_MODEL), lambda i: (i, 0)),
        out_shape=jax.ShapeDtypeStruct((n, D_MODEL), F32),
        compiler_params=_cparams(1),
        name="conformer",
    )(x2d, x2d, x2d, g, shift, scale, gate, w1, b1, fwd, inv, gspec, bdw, lng, lnb, w2, b2)


def kernel(x, c, ctx, c_ctx, w_ada, b_ada, g_mix, g_ffn, w_in_hyb, q_gain, k_gain, w_out_hyb,
           w_pw1, b_pw1, w_cdw, b_cdw, ln_g, ln_b, w_pw2, b_pw2, w_up, w_fdw, b_fdw, w_down):
    batch, seq, d = x.shape
    assert batch == 1 and d == D_MODEL
    x2d = x.reshape(seq, d)
    ctx2d = ctx.reshape(-1, d)
    row = lambda v: v.reshape(1, -1)

    cond = jnp.zeros((SUBLANES, d), F32).at[0].set(c[0]).at[1].set(c_ctx)
    mods = _ada(cond, w_ada, b_ada)
    mod = lambda layer, who, j: mods[layer, who:who + 1, j * d:(j + 1) * d]

    w_in = w_in_hyb[0].astype(BF16)
    qg = row(jnp.tile(q_gain[0], LANES // HEAD_DIM))
    kg = row(jnp.tile(k_gain[0], LANES // HEAD_DIM))
    bd = _headnorm_matrix()
    dft = _channel_dft_matrix()
    shift2 = mods[0, 0:2, 0:d].reshape(2, 1, d)
    scale2 = mods[0, 0:2, d:2 * d].reshape(2, 1, d)
    qt, k_h, vt, ab = _inproj(x2d, ctx2d, row(g_mix[0]), shift2, scale2, w_in, qg, kg,
                              _rope_tables(seq), bd, dft)
    score_bound = HEAD_DIM * Q_SCALE * jnp.max(jnp.abs(q_gain[0])) * jnp.max(jnp.abs(k_gain[0]))
    att = _attention(qt, k_h, vt, score_bound)
    fm = _fourier(ab)
    ffn_w = (w_up.astype(BF16), w_fdw, b_fdw.reshape(b_fdw.shape[0], 1, -1), w_down.astype(BF16))
    x2 = _mix_ffn(x2d, att, fm, w_out_hyb[0].astype(BF16), mod(0, 0, 2),
                  row(g_ffn[0]), mod(0, 0, 3), mod(0, 0, 4), mod(0, 0, 5), 0, *ffn_w)

    x3 = _conformer(x2, row(g_mix[1]), mod(1, 0, 0), mod(1, 0, 1), mod(1, 0, 2),
                    w_pw1[0].astype(BF16), row(b_pw1[0]), w_cdw[0], row(b_cdw[0]),
                    row(ln_g[0]), row(ln_b[0]), w_pw2[0].astype(BF16), row(b_pw2[0]))
    x4 = _ffn(x3, row(g_ffn[1]), mod(1, 0, 3), mod(1, 0, 4), mod(1, 0, 5), 1, *ffn_w)
    return x4.reshape(batch, seq, d)
```

```python
import functools
import math

import numpy as np
import jax
import jax.numpy as jnp
from jax import lax
from jax.experimental import pallas as pl
from jax.experimental.pallas import tpu as pltpu

F32 = jnp.float32
BF16 = jnp.bfloat16

D_MODEL = 1024
GRID_W = 64
HEAD_DIM = 64
N_Q_HEADS = 8
N_KV_HEADS = 2
GQA_GROUP = N_Q_HEADS // N_KV_HEADS
Q_W = N_Q_HEADS * HEAD_DIM
KV_W = N_KV_HEADS * HEAD_DIM
F_GROUPS = 8
F_GROUP_DIM = 64
F_W = F_GROUPS * F_GROUP_DIM
HYB_IN = Q_W + 2 * KV_W + F_W
ROPE_HALF = HEAD_DIM // 2
ROPE_THETA = 10000.0
CONV_WIDTH = 31
FFN_DIM = 2816
NORM_EPS = 1e-6
LN_EPS = 1e-5

LANES = 128
SUBLANES = 8
FFT_N = 128
V_ROWS = 80
VMEM_LIMIT = 60 * 1024 * 1024

Q_SCALE = HEAD_DIM ** -0.5 * math.log2(math.e)


def _cparams(n_axes=1):
    return pltpu.CompilerParams(dimension_semantics=("arbitrary",) * n_axes,
                                vmem_limit_bytes=VMEM_LIMIT)


def _const_spec(shape):
    zeros = (0,) * len(shape)
    return pl.BlockSpec(shape, lambda *_: zeros, pipeline_mode=pl.Buffered(1))


def _rms_mod(x, g, shift, scale):
    ms = jnp.mean(x * x, axis=-1, keepdims=True)
    return (x * lax.rsqrt(ms + NORM_EPS) * g) * (1.0 + scale) + shift


def _silu(x):
    return x * (1.0 / (1.0 + jnp.exp(-x)))


def _ada_kernel(cond_ref, w_ref, b_ref, o_ref):
    cnd = cond_ref[...]
    o_ref[0] = jnp.dot(_silu(cnd), w_ref[0], preferred_element_type=F32,
                       precision=lax.Precision.HIGHEST) + b_ref[0]


def _ada(cond, w_ada, b_ada):
    depth, d, n = w_ada.shape
    tn = 1536
    return pl.pallas_call(
        _ada_kernel,
        grid=(depth, n // tn),
        in_specs=[pl.BlockSpec((SUBLANES, d), lambda i, j: (0, 0)),
                  pl.BlockSpec((1, d, tn), lambda i, j: (i, 0, j)),
                  pl.BlockSpec((1, 1, tn), lambda i, j: (i, 0, j))],
        out_specs=pl.BlockSpec((1, SUBLANES, tn), lambda i, j: (i, 0, j)),
        out_shape=jax.ShapeDtypeStruct((depth, SUBLANES, n), F32),
        compiler_params=_cparams(2),
        name="ada",
    )(cond, w_ada, b_ada.reshape(depth, 1, n))


INPROJ_TM = ATTN_TQ = 256
GRID_ROWS_PER_TILE = INPROJ_TM // GRID_W


def _inproj_kernel(x_ref, ctx_ref, gm_ref, sh_ref, sc_ref, w_ref, qg_ref, kg_ref, rcos_ref, rsin_ref,
                   ccos_ref, csin_ref, bd_ref, dft_ref, qt_ref, k_ref, vt_ref, ab_ref, *, n_lat):
    i = pl.program_id(0)
    is_ctx = i >= n_lat
    xin = jnp.where(is_ctx, ctx_ref[...], x_ref[...])
    h = _rms_mod(xin, gm_ref[...], sh_ref[...], sc_ref[...])
    u = jnp.dot(h, w_ref[...], preferred_element_type=F32)

    def rope_table(row_ref, col_ref, ctx_value):
        rows = [jnp.tile(row_ref[g * SUBLANES:(g + 1) * SUBLANES, :], (GRID_W // SUBLANES, 1))
                for g in range(GRID_ROWS_PER_TILE)]
        t = jnp.concatenate(rows, axis=0) + jnp.tile(col_ref[...], (GRID_ROWS_PER_TILE, 1))
        return jnp.where(is_ctx, ctx_value, t)

    cos = rope_table(rcos_ref, ccos_ref, 1.0)
    sin = rope_table(rsin_ref, csin_ref, 0.0)
    bd = bd_ref[...]
    lane = lax.broadcasted_iota(jnp.int32, cos.shape, 1)
    first = (lane % ROPE_HALF) < (ROPE_HALF // 2)

    def head_norm_rope(t, gain):
        sq = t * t
        hi = sq.astype(BF16)
        lo = (sq - hi.astype(F32)).astype(BF16)
        ms = (jnp.dot(hi, bd, preferred_element_type=F32)
              + jnp.dot(lo, bd, preferred_element_type=F32))
        tn = t * lax.rsqrt(ms + NORM_EPS) * gain
        partner = jnp.where(first, pltpu.roll(tn, LANES - ROPE_HALF // 2, 1),
                            pltpu.roll(tn, ROPE_HALF // 2, 1))
        return tn * cos + partner * sin

    kk = head_norm_rope(u[:, Q_W:Q_W + KV_W], kg_ref[...]).astype(BF16)
    vt = u[:, Q_W + KV_W:Q_W + 2 * KV_W].T
    pad_rows = lax.broadcasted_iota(jnp.int32, (V_ROWS - HEAD_DIM, INPROJ_TM), 0)
    ones_rows = jnp.where(pad_rows == 0, 1.0, 0.0).astype(BF16)
    for kv in range(N_KV_HEADS):
        k_ref[kv] = kk[:, kv * HEAD_DIM:(kv + 1) * HEAD_DIM]
        vt_ref[kv, :HEAD_DIM, :] = vt[kv * HEAD_DIM:(kv + 1) * HEAD_DIM, :].astype(BF16)
        vt_ref[kv, HEAD_DIM:, :] = ones_rows

    @pl.when(i < n_lat)
    def _():
        qg = qg_ref[...]
        for j in range(Q_W // LANES):
            t = head_norm_rope(u[:, j * LANES:(j + 1) * LANES], qg) * Q_SCALE
            tt = t.T.astype(BF16)
            for hh in range(LANES // HEAD_DIM):
                head = j * (LANES // HEAD_DIM) + hh
                kv, g = head // GQA_GROUP, head % GQA_GROUP
                qt_ref[kv, :, g * INPROJ_TM:(g + 1) * INPROJ_TM] = tt[hh * HEAD_DIM:(hh + 1) * HEAD_DIM, :]
        f = u[:, Q_W + 2 * KV_W:].astype(BF16)
        ab_ref[...] = jnp.dot(f, dft_ref[...], preferred_element_type=F32).astype(BF16)


def _inproj(x2d, ctx2d, gm, shift2, scale2, w_in, qg, kg, rope, bd, dft):
    n = x2d.shape[0]
    tm = INPROJ_TM
    n_lat = n // tm
    assert ctx2d.shape[0] == tm and n % tm == 0
    lk = n + tm
    lat = lambda i: jnp.minimum(i, n_lat - 1)
    vec = lambda w: _const_spec((1, w))
    mod = pl.BlockSpec((None, 1, D_MODEL), lambda i: (i // n_lat, 0, 0))
    rtab = pl.BlockSpec((GRID_ROWS_PER_TILE * SUBLANES, LANES), lambda i: (lat(i), 0))
    return pl.pallas_call(
        functools.partial(_inproj_kernel, n_lat=n_lat),
        grid=(n_lat + 1,),
        in_specs=[pl.BlockSpec((tm, D_MODEL), lambda i: (lat(i), 0)), _const_spec((tm, D_MODEL)),
                  vec(D_MODEL), mod, mod, _const_spec((D_MODEL, HYB_IN)), vec(LANES), vec(LANES),
                  rtab, rtab, _const_spec((GRID_W, LANES)), _const_spec((GRID_W, LANES)),
                  _const_spec((LANES, LANES)), _const_spec((F_W, 2 * F_W))],
        out_specs=[pl.BlockSpec((N_KV_HEADS, None, HEAD_DIM, GQA_GROUP * tm), lambda i: (0, lat(i), 0, 0)),
                   pl.BlockSpec((N_KV_HEADS, tm, HEAD_DIM), lambda i: (0, i, 0)),
                   pl.BlockSpec((N_KV_HEADS, V_ROWS, tm), lambda i: (0, 0, i)),
                   pl.BlockSpec((tm, 2 * F_W), lambda i: (lat(i), 0))],
        out_shape=[jax.ShapeDtypeStruct((N_KV_HEADS, n_lat, HEAD_DIM, GQA_GROUP * tm), BF16),
                   jax.ShapeDtypeStruct((N_KV_HEADS, lk, HEAD_DIM), BF16),
                   jax.ShapeDtypeStruct((N_KV_HEADS, V_ROWS, lk), BF16),
                   jax.ShapeDtypeStruct((n, 2 * F_W), BF16)],
        compiler_params=_cparams(1),
        name="inproj",
    )(x2d, ctx2d, gm, shift2, scale2, w_in, qg, kg, *rope, bd, dft)


def _rope_tables(n):
    inv_freq = 1.0 / (ROPE_THETA ** (np.arange(0, ROPE_HALF, 2, dtype=np.float64) / ROPE_HALF))
    rows = n // GRID_W
    e = np.arange(ROPE_HALF)
    sign = np.where(e < ROPE_HALF // 2, -1.0, 1.0)
    ang_r = np.arange(rows)[:, None] * inv_freq[e % (ROPE_HALF // 2)][None, :]
    ang_c = np.arange(GRID_W)[:, None] * inv_freq[e % (ROPE_HALF // 2)][None, :]

    def two_heads(row_half, col_half):
        t = np.concatenate([row_half, col_half], axis=1)
        return np.concatenate([t, t], axis=1)

    zr, zc = np.zeros_like(ang_r), np.zeros_like(ang_c)
    rep = lambda t: jnp.asarray(np.repeat(t, SUBLANES, axis=0), F32)
    return (rep(two_heads(np.cos(ang_r), zr)), rep(two_heads(np.sin(ang_r) * sign, zr)),
            jnp.asarray(two_heads(zc, np.cos(ang_c)), F32),
            jnp.asarray(two_heads(zc, np.sin(ang_c) * sign), F32))


def _headnorm_matrix():
    i = np.arange(LANES)
    return jnp.asarray((i[:, None] // HEAD_DIM == i[None, :] // HEAD_DIM) / HEAD_DIM, BF16)


def _channel_dft_matrix():
    i = np.arange(F_W)
    same = (i[:, None] // F_GROUP_DIM == i[None, :] // F_GROUP_DIM)
    ang = 2.0 * np.pi * ((i[:, None] % F_GROUP_DIM) * (i[None, :] % F_GROUP_DIM) % F_GROUP_DIM) / F_GROUP_DIM
    s = F_GROUP_DIM ** -0.5
    return jnp.asarray(np.concatenate([np.cos(ang) * same * s, np.sin(ang) * same * s], axis=1), BF16)


ATTN_TK = 256
ATTN_UNROLL = 65
ATTN_SLOT = tuple(s % 6 for s in range(ATTN_UNROLL))
ATTN_NSLOT = max(ATTN_SLOT) + 1
ATTN_UNSHIFTED_LIMIT = 80.0
ATTN_M_INIT = -1e30


def _attn_kernel(qt_ref, k_ref, vt_ref, o_ref, *scr, tk, unroll, running_max):
    s_scr = scr[:ATTN_NSLOT]
    p_scr = scr[ATTN_NSLOT:2 * ATTN_NSLOT]
    acc_scr = scr[2 * ATTN_NSLOT]
    if running_max:
        m_scr = scr[2 * ATTN_NSLOT + 1]
        a_scr = scr[2 * ATTN_NSLOT + 2:]
    slot_of = lambda s: ATTN_SLOT[s % unroll]
    nb = k_ref.shape[0] // tk
    qt = qt_ref[...]

    def scores(blk, slot):
        off = pl.multiple_of(blk * tk, tk)
        s_scr[slot][...] = jnp.dot(k_ref[pl.ds(off, tk), :], qt, preferred_element_type=F32)

    def probs(slot):
        s = s_scr[slot][...]
        if running_max:
            m_old = m_scr[...]
            m_new = jnp.maximum(m_old, jnp.max(s, axis=0, keepdims=True))
            a_scr[slot][...] = jnp.exp2(m_old - m_new)
            m_scr[...] = m_new
            s = s - m_new
        p_scr[slot][...] = jnp.exp2(s).astype(BF16)

    def accumulate(blk, slot):
        off = pl.multiple_of(blk * tk, tk)
        d = jnp.dot(vt_ref[:, pl.ds(off, tk)], p_scr[slot][...], preferred_element_type=F32)
        if running_max:
            acc_scr[...] = acc_scr[...] * a_scr[slot][...] + d
        else:
            acc_scr[...] += d

    def steps(j, n_probs, n_scores):
        for s in range(unroll):
            accumulate(j * unroll + s, slot_of(s))
            if s < n_probs:
                probs(slot_of(s + 1))
            if s < n_scores:
                scores(j * unroll + s + 2, slot_of(s + 2))

    def body(j, carry):
        steps(j, unroll, unroll)
        return carry

    acc_scr[...] = jnp.zeros_like(acc_scr)
    if running_max:
        m_scr[...] = jnp.full_like(m_scr, ATTN_M_INIT)
    scores(0, slot_of(0))
    scores(1, slot_of(1))
    probs(slot_of(0))
    n_iter = nb // unroll
    lax.fori_loop(0, n_iter - 1, body, 0)
    steps(n_iter - 1, unroll - 1, unroll - 2)
    acc = acc_scr[...]
    o = acc[:HEAD_DIM, :] * (1.0 / acc[HEAD_DIM:HEAD_DIM + 1, :])
    tq = o.shape[1] // GQA_GROUP
    outs = [o[:, g * tq:(g + 1) * tq].T for g in range(GQA_GROUP)]
    o_ref[...] = jnp.concatenate(outs, axis=1).astype(o_ref.dtype)


def _attention_call(qt, k, vt, *, running_max):
    tq, tk, unroll = ATTN_TQ, ATTN_TK, ATTN_UNROLL
    nq = qt.shape[1]
    lk = k.shape[1]
    assert lk % (tk * unroll) == 0 and unroll >= 3
    gw = GQA_GROUP * HEAD_DIM
    nlane = GQA_GROUP * tq
    scratch = ([pltpu.VMEM((tk, nlane), F32)] * ATTN_NSLOT
               + [pltpu.VMEM((tk, nlane), BF16)] * ATTN_NSLOT
               + [pltpu.VMEM((V_ROWS, nlane), F32)])
    if running_max:
        scratch += [pltpu.VMEM((1, nlane), F32)] * (1 + ATTN_NSLOT)
    return pl.pallas_call(
        functools.partial(_attn_kernel, tk=tk, unroll=unroll, running_max=running_max),
        grid=(N_KV_HEADS, nq),
        in_specs=[pl.BlockSpec((None, None, HEAD_DIM, nlane), lambda h, i: (h, i, 0, 0)),
                  pl.BlockSpec((None, lk, HEAD_DIM), lambda h, i: (h, 0, 0)),
                  pl.BlockSpec((None, V_ROWS, lk), lambda h, i: (h, 0, 0))],
        out_specs=pl.BlockSpec((tq, gw), lambda h, i: (i, h)),
        out_shape=jax.ShapeDtypeStruct((nq * tq, Q_W), BF16),
        scratch_shapes=scratch,
        compiler_params=_cparams(2),
        name="attention_running_max" if running_max else "attention",
    )(qt, k, vt)


def _attention(qt, k, vt, score_bound):
    return lax.cond(score_bound < ATTN_UNSHIFTED_LIMIT,
                    functools.partial(_attention_call, running_max=False),
                    functools.partial(_attention_call, running_max=True), qt, k, vt)


def _fft1_kernel(ab_ref, ca_ref, cb_ref, tr_ref, ti_ref, *, nblk):
    ca = ca_ref[...]
    cb = cb_ref[...]
    for j in range(nblk):
        a = ab_ref[:, j * 2 * F_W:j * 2 * F_W + F_W]
        b = ab_ref[:, j * 2 * F_W + F_W:(j + 1) * 2 * F_W]
        t = (jnp.dot(ca, a, preferred_element_type=F32)
             + jnp.dot(cb, b, preferred_element_type=F32))
        tr_ref[:, j * F_W:(j + 1) * F_W] = t[:FFT_N].astype(BF16)
        ti_ref[:, j * F_W:(j + 1) * F_W] = t[FFT_N:].astype(BF16)


def _fft2_kernel(tr_ref, ti_ref, m_ref, y_ref, *, nblk):
    for j in range(nblk):
        m = m_ref[j]
        y = (jnp.dot(m[:, :FFT_N], tr_ref[j * FFT_N:(j + 1) * FFT_N, :], preferred_element_type=F32)
             + jnp.dot(m[:, FFT_N:], ti_ref[j * FFT_N:(j + 1) * FFT_N, :], preferred_element_type=F32))
        y_ref[:, j * F_W:(j + 1) * F_W] = y.astype(BF16)


def _fft_tables():
    n = FFT_N
    l = n * n
    s = n ** -0.5
    k = np.arange(n)
    ang1 = 2.0 * np.pi * ((k[:, None] * k[None, :]) % n) / n
    c1, s1 = np.cos(ang1) * s, np.sin(ang1) * s
    ca = np.concatenate([c1, -s1], axis=0)
    cb = np.concatenate([-s1, -c1], axis=0)
    k1 = k[:, None, None]
    k2 = k[None, :, None]
    n2 = k[None, None, :]
    ang2 = 2.0 * np.pi * ((n2 * (k1 + n * k2)) % l) / l
    m = np.concatenate([np.cos(ang2) * s, np.sin(ang2) * s], axis=2)
    return jnp.asarray(ca, BF16), jnp.asarray(cb, BF16), jnp.asarray(m, BF16)


def _fourier(ab):
    l = ab.shape[0]
    assert l == FFT_N * FFT_N
    ca, cb, m = _fft_tables()
    nblk = 16
    ab_v = ab.reshape(FFT_N, FFT_N * 2 * F_W)
    tr, ti = pl.pallas_call(
        functools.partial(_fft1_kernel, nblk=nblk),
        grid=(FFT_N // nblk,),
        in_specs=[pl.BlockSpec((FFT_N, nblk * 2 * F_W), lambda i: (0, i)),
                  _const_spec((2 * FFT_N, FFT_N)), _const_spec((2 * FFT_N, FFT_N))],
        out_specs=[pl.BlockSpec((FFT_N, nblk * F_W), lambda i: (0, i))] * 2,
        out_shape=[jax.ShapeDtypeStruct((FFT_N, FFT_N * F_W), BF16)] * 2,
        compiler_params=_cparams(1),
        name="fft_stage1",
    )(ab_v, ca, cb)
    tr = tr.reshape(l, F_W)
    ti = ti.reshape(l, F_W)
    y = pl.pallas_call(
        functools.partial(_fft2_kernel, nblk=nblk),
        grid=(FFT_N // nblk,),
        in_specs=[pl.BlockSpec((nblk * FFT_N, F_W), lambda i: (i, 0)),
                  pl.BlockSpec((nblk * FFT_N, F_W), lambda i: (i, 0)),
                  pl.BlockSpec((nblk, FFT_N, 2 * FFT_N), lambda i: (i, 0, 0))],
        out_specs=pl.BlockSpec((FFT_N, nblk * F_W), lambda i: (0, i)),
        out_shape=jax.ShapeDtypeStruct((FFT_N, FFT_N * F_W), BF16),
        compiler_params=_cparams(1),
        name="fft_stage2",
    )(tr, ti, m)
    return y.reshape(l, F_W)


FFN_TM = 512
FFN_HALO = SUBLANES
FFN_CHUNKS = 1
BF16_ROWS = 2 * SUBLANES


def _ffn_tile(xp, x, xn, g_ref, sh_ref, sc_ref, gate_ref, wup_ref, wdw_ref, bdw_ref, wdn_ref):
    i = pl.program_id(0)
    last = pl.num_programs(0) - 1
    tm = x.shape[0]
    g, sh, sc = g_ref[...], sh_ref[...], sc_ref[...]
    hp = _rms_mod(xp, g, sh, sc) * jnp.where(i > 0, 1.0, 0.0)
    hn = _rms_mod(xn, g, sh, sc) * jnp.where(i < last, 1.0, 0.0)
    h = jnp.concatenate([hp, _rms_mod(x, g, sh, sc), hn], axis=0)
    rows = tm + 2 * FFN_HALO
    cw = FFN_DIM // FFN_CHUNKS

    def conv(u, c0):
        w = wdw_ref[:, c0:c0 + cw]
        um = pltpu.roll(u, 1, 0)[FFN_HALO:FFN_HALO + tm]
        up = pltpu.roll(u, rows - 1, 0)[FFN_HALO:FFN_HALO + tm]
        return (um * w[0:1] + u[FFN_HALO:FFN_HALO + tm] * w[1:2] + up * w[2:3]
                + bdw_ref[:, c0:c0 + cw])

    acc = None
    for c in range(FFN_CHUNKS):
        ca, cb = c * cw, FFN_DIM + c * cw
        a = conv(jnp.dot(h, wup_ref[:, ca:ca + cw], preferred_element_type=F32), ca)
        b = conv(jnp.dot(h, wup_ref[:, cb:cb + cw], preferred_element_type=F32), cb)
        act = _silu(a) * b
        d = jnp.dot(act, wdn_ref[ca:ca + cw, :], preferred_element_type=F32)
        acc = d if acc is None else acc + d
    return x + gate_ref[...] * acc


def _ffn_kernel(xp_ref, x_ref, xn_ref, *rest):
    *ffn_refs, o_ref = rest
    o_ref[...] = _ffn_tile(xp_ref[...], x_ref[...], xn_ref[...], *ffn_refs)


def _mix_ffn_kernel(xp_ref, x_ref, xn_ref, ap_ref, a_ref, an_ref, yp_ref, y_ref, yn_ref, wo_ref, g1_ref,
                    *rest):
    *ffn_refs, o_ref = rest
    tm = x_ref.shape[0]
    a_all = jnp.concatenate([a_ref[...], ap_ref[...], an_ref[...]], axis=0)
    y_all = jnp.concatenate([y_ref[...], yp_ref[...], yn_ref[...]], axis=0)
    o_all = (jnp.dot(a_all, wo_ref[:Q_W, :], preferred_element_type=F32)
             + jnp.dot(y_all, wo_ref[Q_W:, :], preferred_element_type=F32))
    g1 = g1_ref[...]
    prev_lo = tm + BF16_ROWS - FFN_HALO
    next_lo = tm + BF16_ROWS
    x1 = x_ref[...] + g1 * o_all[:tm]
    x1p = xp_ref[...] + g1 * o_all[prev_lo:prev_lo + FFN_HALO]
    x1n = xn_ref[...] + g1 * o_all[next_lo:next_lo + FFN_HALO]
    o_ref[...] = _ffn_tile(x1p, x1, x1n, *ffn_refs)


def _halo_specs(n, tm, halo, width):
    hb, nh = tm // halo, n // halo
    return [pl.BlockSpec((halo, width), lambda i: (jnp.maximum(i * hb - 1, 0), 0)),
            pl.BlockSpec((tm, width), lambda i: (i, 0)),
            pl.BlockSpec((halo, width), lambda i: (jnp.minimum((i + 1) * hb, nh - 1), 0))]


def _ffn_weight_specs(layer):
    vec = lambda: _const_spec((1, D_MODEL))
    per_layer = lambda r, c: pl.BlockSpec((None, r, c), lambda i: (layer, 0, 0),
                                          pipeline_mode=pl.Buffered(1))
    return [vec(), vec(), vec(), vec(),
            per_layer(D_MODEL, 2 * FFN_DIM), per_layer(3, 2 * FFN_DIM),
            per_layer(1, 2 * FFN_DIM), per_layer(FFN_DIM, D_MODEL)]


def _ffn(x2d, g, shift, scale, gate, layer, w_up, w_dw, b_dw, w_down):
    n = x2d.shape[0]
    tm = FFN_TM
    return pl.pallas_call(
        _ffn_kernel,
        grid=(n // tm,),
        in_specs=_halo_specs(n, tm, FFN_HALO, D_MODEL) + _ffn_weight_specs(layer),
        out_specs=pl.BlockSpec((tm, D_MODEL), lambda i: (i, 0)),
        out_shape=jax.ShapeDtypeStruct((n, D_MODEL), F32),
        compiler_params=_cparams(1),
        name="ffn",
    )(x2d, x2d, x2d, g, shift, scale, gate, w_up, w_dw, b_dw, w_down)


def _mix_ffn(x2d, att, fm, w_out, gate1, g, shift, scale, gate, layer, w_up, w_dw, b_dw, w_down):
    n = x2d.shape[0]
    tm = FFN_TM
    return pl.pallas_call(
        _mix_ffn_kernel,
        grid=(n // tm,),
        in_specs=(_halo_specs(n, tm, FFN_HALO, D_MODEL) + _halo_specs(n, tm, BF16_ROWS, Q_W)
                  + _halo_specs(n, tm, BF16_ROWS, F_W)
                  + [_const_spec((Q_W + F_W, D_MODEL)), _const_spec((1, D_MODEL))]
                  + _ffn_weight_specs(layer)),
        out_specs=pl.BlockSpec((tm, D_MODEL), lambda i: (i, 0)),
        out_shape=jax.ShapeDtypeStruct((n, D_MODEL), F32),
        compiler_params=_cparams(1),
        name="mix_ffn",
    )(x2d, x2d, x2d, att, att, att, fm, fm, fm, w_out, gate1, g, shift, scale, gate,
      w_up, w_dw, b_dw, w_down)


CONF_HALO = 2 * SUBLANES
CONF_TM = 512
CONF_N = CONF_TM + 2 * CONF_HALO
CONF_NF = 288


def _conv_dft_tables():
    n, nf = CONF_N, CONF_N // 2 + 1
    f = np.arange(CONF_NF)[:, None]
    live = f < nf
    t = np.arange(n)[None, :]
    ang = 2.0 * np.pi * ((f * t) % n) / n
    fwd = np.concatenate([np.cos(ang) * live, np.sin(ang) * live], axis=0)
    r = np.arange(CONF_HALO, CONF_HALO + CONF_TM)[:, None]
    fi = f.T
    weight = np.where((fi == 0) | (fi == n // 2), 1.0, 2.0) * live.T / n
    angi = 2.0 * np.pi * ((r * fi) % n) / n
    inv = np.concatenate([weight * np.cos(angi), -weight * np.sin(angi)], axis=1)
    k = np.arange(CONV_WIDTH)[None, :] - (CONV_WIDTH - 1) // 2
    angw = 2.0 * np.pi * ((f * k) % n) / n
    gw = np.concatenate([np.cos(angw) * live, np.sin(angw) * live], axis=0)
    return jnp.asarray(fwd, BF16), jnp.asarray(inv, BF16), jnp.asarray(gw, F32)


def _filter_spectrum_kernel(gw_ref, w_ref, o_ref):
    o_ref[...] = jnp.dot(gw_ref[...], w_ref[...], preferred_element_type=F32,
                         precision=lax.Precision.HIGHEST)


def _filter_spectrum(gw, wdw):
    return pl.pallas_call(
        _filter_spectrum_kernel,
        out_shape=jax.ShapeDtypeStruct((2 * CONF_NF, D_MODEL), F32),
        name="conv_filter_spectrum",
    )(gw, wdw)


def _conf_kernel(xp_ref, x_ref, xn_ref, g_ref, sh_ref, sc_ref, gate_ref, w1_ref, b1_ref, fwd_ref, inv_ref,
                 gs_ref, bdw_ref, lng_ref, lnb_ref, w2_ref, b2_ref, o_ref, *, seq_len):
    i = pl.program_id(0)
    tm = x_ref.shape[0]
    g, sh, sc = g_ref[...], sh_ref[...], sc_ref[...]
    x = x_ref[...]
    h = jnp.concatenate([_rms_mod(xp_ref[...], g, sh, sc), _rms_mod(x, g, sh, sc),
                         _rms_mod(xn_ref[...], g, sh, sc)], axis=0)
    u = jnp.dot(h, w1_ref[...], preferred_element_type=F32) + b1_ref[...]
    glu = u[:, :D_MODEL] * (1.0 / (1.0 + jnp.exp(-u[:, D_MODEL:])))
    pos = i * tm - CONF_HALO + lax.broadcasted_iota(jnp.int32, (CONF_N, 1), 0)
    glu = jnp.where((pos >= 0) & (pos < seq_len), glu, 0.0).astype(BF16)
    spec = jnp.dot(fwd_ref[...], glu, preferred_element_type=F32)
    uc, us = spec[:CONF_NF], spec[CONF_NF:]
    gc, gs = gs_ref[:CONF_NF, :], gs_ref[CONF_NF:, :]
    y = jnp.concatenate([uc * gc + us * gs, uc * gs - us * gc], axis=0).astype(BF16)
    acc = jnp.dot(inv_ref[...], y, preferred_element_type=F32) + bdw_ref[...]
    mu = jnp.mean(acc, axis=-1, keepdims=True)
    xc = acc - mu
    var = jnp.mean(xc * xc, axis=-1, keepdims=True)
    yn = xc * lax.rsqrt(var + LN_EPS) * lng_ref[...] + lnb_ref[...]
    o = jnp.dot(_silu(yn), w2_ref[...], preferred_element_type=F32) + b2_ref[...]
    o_ref[...] = x + gate_ref[...] * o


def _conformer(x2d, g, shift, scale, gate, w1, b1, wdw, bdw, lng, lnb, w2, b2):
    n = x2d.shape[0]
    tm = CONF_TM
    hb = tm // CONF_HALO
    nh = n // CONF_HALO
    fwd, inv, gw = _conv_dft_tables()
    gspec = _filter_spectrum(gw, wdw)
    vec = lambda w=D_MODEL: _const_spec((1, w))
    return pl.pallas_call(
        functools.partial(_conf_kernel, seq_len=n),
        grid=(n // tm,),
        in_specs=[pl.BlockSpec((CONF_HALO, D_MODEL), lambda i: (jnp.maximum(i * hb - 1, 0), 0)),
                  pl.BlockSpec((tm, D_MODEL), lambda i: (i, 0)),
                  pl.BlockSpec((CONF_HALO, D_MODEL), lambda i: (jnp.minimum((i + 1) * hb, nh - 1), 0)),
                  vec(), vec(), vec(), vec(),
                  _const_spec((D_MODEL, 2 * D_MODEL)), vec(2 * D_MODEL),
                  _const_spec((2 * CONF_NF, CONF_N)), _const_spec((tm, 2 * CONF_NF)),
                  _const_spec((2 * CONF_NF, D_MODEL)), vec(), vec(), vec(),
                  _const_spec((D_MODEL, D_MODEL)), vec()],
        out_specs=pl.BlockSpec((tm, D_MODEL), lambda i: (i, 0)),
        out_shape=jax.ShapeDtypeStruct((n, D_MODEL), F32),
        compiler_params=_cparams(1),
        name="conformer",
    )(x2d, x2d, x2d, g, shift, scale, gate, w1, b1, fwd, inv, gspec, bdw, lng, lnb, w2, b2)


def kernel(x, c, ctx, c_ctx, w_ada, b_ada, g_mix, g_ffn, w_in_hyb, q_gain, k_gain, w_out_hyb,
           w_pw1, b_pw1, w_cdw, b_cdw, ln_g, ln_b, w_pw2, b_pw2, w_up, w_fdw, b_fdw, w_down):
    batch, seq, d = x.shape
    assert batch == 1 and d == D_MODEL
    x2d = x.reshape(seq, d)
    ctx2d = ctx.reshape(-1, d)
    row = lambda v: v.reshape(1, -1)

    cond = jnp.zeros((SUBLANES, d), F32).at[0].set(c[0]).at[1].set(c_ctx)
    mods = _ada(cond, w_ada, b_ada)
    mod = lambda layer, who, j: mods[layer, who:who + 1, j * d:(j + 1) * d]

    w_in = w_in_hyb[0]
    qg = row(jnp.tile(q_gain[0], LANES // HEAD_DIM))
    kg = row(jnp.tile(k_gain[0], LANES // HEAD_DIM))
    bd = _headnorm_matrix()
    dft = _channel_dft_matrix()
    shift2 = mods[0, 0:2, 0:d].reshape(2, 1, d)
    scale2 = mods[0, 0:2, d:2 * d].reshape(2, 1, d)
    qt, k_h, vt, ab = _inproj(x2d, ctx2d, row(g_mix[0]), shift2, scale2, w_in, qg, kg,
                              _rope_tables(seq), bd, dft)
    score_bound = HEAD_DIM * Q_SCALE * jnp.max(jnp.abs(q_gain[0])) * jnp.max(jnp.abs(k_gain[0]))
    att = _attention(qt, k_h, vt, score_bound)
    fm = _fourier(ab)
    ffn_w = (w_up, w_fdw, b_fdw.reshape(b_fdw.shape[0], 1, -1), w_down)
    x2 = _mix_ffn(x2d, att, fm, w_out_hyb[0].astype(BF16), mod(0, 0, 2),
                  row(g_ffn[0]), mod(0, 0, 3), mod(0, 0, 4), mod(0, 0, 5), 0, *ffn_w)

    x3 = _conformer(x2, row(g_mix[1]), mod(1, 0, 0), mod(1, 0, 1), mod(1, 0, 2),
                    w_pw1[0], row(b_pw1[0]), w_cdw[0], row(b_cdw[0]),
                    row(ln_g[0]), row(ln_b[0]), w_pw2[0], row(b_pw2[0]))
    x4 = _ffn(x3, row(g_ffn[1]), mod(1, 0, 3), mod(1, 0, 4), mod(1, 0, 5), 1, *ffn_w)
    return x4.reshape(batch, seq, d)
```

```python
import functools
import math

import numpy as np
import jax
import jax.numpy as jnp
from jax import lax
from jax.experimental import pallas as pl
from jax.experimental.pallas import tpu as pltpu

F32 = jnp.float32
BF16 = jnp.bfloat16

D_MODEL = 1024
GRID_W = 64
HEAD_DIM = 64
N_Q_HEADS = 8
N_KV_HEADS = 2
GQA_GROUP = N_Q_HEADS // N_KV_HEADS
Q_W = N_Q_HEADS * HEAD_DIM
KV_W = N_KV_HEADS * HEAD_DIM
F_GROUPS = 8
F_GROUP_DIM = 64
F_W = F_GROUPS * F_GROUP_DIM
HYB_IN = Q_W + 2 * KV_W + F_W
ROPE_HALF = HEAD_DIM // 2
ROPE_THETA = 10000.0
CONV_WIDTH = 31
FFN_DIM = 2816
NORM_EPS = 1e-6
LN_EPS = 1e-5

LANES = 128
SUBLANES = 8
FFT_N = 128
V_ROWS = 80
VMEM_LIMIT = 60 * 1024 * 1024

Q_SCALE = HEAD_DIM ** -0.5 * math.log2(math.e)


def _cparams(n_axes=1):
    return pltpu.CompilerParams(dimension_semantics=("arbitrary",) * n_axes,
                                vmem_limit_bytes=VMEM_LIMIT)


def _const_spec(shape):
    zeros = (0,) * len(shape)
    return pl.BlockSpec(shape, lambda *_: zeros, pipeline_mode=pl.Buffered(1))


def _rms_mod(x, g, shift, scale):
    ms = jnp.mean(x * x, axis=-1, keepdims=True)
    return (x * lax.rsqrt(ms + NORM_EPS) * g) * (1.0 + scale) + shift


def _silu(x):
    return x * (1.0 / (1.0 + jnp.exp(-x)))


def _ada_kernel(cond_ref, w_ref, b_ref, o_ref):
    def split(v):
        hi = v.astype(BF16)
        return hi, (v - hi.astype(F32)).astype(BF16)

    s_hi, s_lo = split(_silu(cond_ref[...]))
    w_hi, w_lo = split(w_ref[0])
    dot = functools.partial(jnp.dot, preferred_element_type=F32)
    o_ref[0] = dot(s_hi, w_hi) + dot(s_lo, w_hi) + dot(s_hi, w_lo) + b_ref[0]


def _ada(cond, w_ada, b_ada):
    depth, d, n = w_ada.shape
    tn = 1536
    return pl.pallas_call(
        _ada_kernel,
        grid=(depth, n // tn),
        in_specs=[pl.BlockSpec((SUBLANES, d), lambda i, j: (0, 0)),
                  pl.BlockSpec((1, d, tn), lambda i, j: (i, 0, j)),
                  pl.BlockSpec((1, 1, tn), lambda i, j: (i, 0, j))],
        out_specs=pl.BlockSpec((1, SUBLANES, tn), lambda i, j: (i, 0, j)),
        out_shape=jax.ShapeDtypeStruct((depth, SUBLANES, n), F32),
        compiler_params=_cparams(2),
        name="ada",
    )(cond, w_ada, b_ada.reshape(depth, 1, n))


INPROJ_TM = ATTN_TQ = 256
GRID_ROWS_PER_TILE = INPROJ_TM // GRID_W


def _inproj_kernel(x_ref, ctx_ref, gm_ref, sh_ref, sc_ref, w_ref, qg_ref, kg_ref, rcos_ref, rsin_ref,
                   ccos_ref, csin_ref, bd_ref, dft_ref, qt_ref, k_ref, vt_ref, ab_ref, *, n_lat):
    i = pl.program_id(0)
    is_ctx = i >= n_lat
    xin = jnp.where(is_ctx, ctx_ref[...], x_ref[...])
    h = _rms_mod(xin, gm_ref[...], sh_ref[...], sc_ref[...])
    u = jnp.dot(h, w_ref[...], preferred_element_type=F32)

    def rope_table(row_ref, col_ref, ctx_value):
        rows = [jnp.tile(row_ref[g * SUBLANES:(g + 1) * SUBLANES, :], (GRID_W // SUBLANES, 1))
                for g in range(GRID_ROWS_PER_TILE)]
        t = jnp.concatenate(rows, axis=0) + jnp.tile(col_ref[...], (GRID_ROWS_PER_TILE, 1))
        return jnp.where(is_ctx, ctx_value, t)

    cos = rope_table(rcos_ref, ccos_ref, 1.0)
    sin = rope_table(rsin_ref, csin_ref, 0.0)
    bd = bd_ref[...]
    lane = lax.broadcasted_iota(jnp.int32, cos.shape, 1)
    first = (lane % ROPE_HALF) < (ROPE_HALF // 2)

    def head_norm_rope(t, gain):
        sq = t * t
        hi = sq.astype(BF16)
        lo = (sq - hi.astype(F32)).astype(BF16)
        ms = jnp.dot(jnp.concatenate([hi, lo], axis=1), bd, preferred_element_type=F32)
        tn = t * lax.rsqrt(ms + NORM_EPS) * gain
        partner = jnp.where(first, pltpu.roll(tn, LANES - ROPE_HALF // 2, 1),
                            pltpu.roll(tn, ROPE_HALF // 2, 1))
        return tn * cos + partner * sin

    kk = head_norm_rope(u[:, Q_W:Q_W + KV_W], kg_ref[...]).astype(BF16)
    vt = u[:, Q_W + KV_W:Q_W + 2 * KV_W].T
    pad_rows = lax.broadcasted_iota(jnp.int32, (V_ROWS - HEAD_DIM, INPROJ_TM), 0)
    ones_rows = jnp.where(pad_rows == 0, 1.0, 0.0).astype(BF16)
    for kv in range(N_KV_HEADS):
        k_ref[kv] = kk[:, kv * HEAD_DIM:(kv + 1) * HEAD_DIM]
        vt_ref[kv, :HEAD_DIM, :] = vt[kv * HEAD_DIM:(kv + 1) * HEAD_DIM, :].astype(BF16)
        vt_ref[kv, HEAD_DIM:, :] = ones_rows

    @pl.when(i < n_lat)
    def _():
        qg = qg_ref[...]
        for j in range(Q_W // LANES):
            t = head_norm_rope(u[:, j * LANES:(j + 1) * LANES], qg) * Q_SCALE
            tt = t.astype(BF16).T
            for hh in range(LANES // HEAD_DIM):
                head = j * (LANES // HEAD_DIM) + hh
                kv, g = head // GQA_GROUP, head % GQA_GROUP
                qt_ref[kv, :, g * INPROJ_TM:(g + 1) * INPROJ_TM] = tt[hh * HEAD_DIM:(hh + 1) * HEAD_DIM, :]
        f = u[:, Q_W + 2 * KV_W:].astype(BF16)
        ab_ref[...] = jnp.dot(f, dft_ref[...], preferred_element_type=F32).astype(BF16)


def _inproj(x2d, ctx2d, gm, shift2, scale2, w_in, qg, kg, rope, bd, dft):
    n = x2d.shape[0]
    tm = INPROJ_TM
    n_lat = n // tm
    assert ctx2d.shape[0] == tm and n % tm == 0
    lk = n + tm
    lat = lambda i: jnp.minimum(i, n_lat - 1)
    vec = lambda w: _const_spec((1, w))
    mod = pl.BlockSpec((None, 1, D_MODEL), lambda i: (i // n_lat, 0, 0))
    rtab = pl.BlockSpec((GRID_ROWS_PER_TILE * SUBLANES, LANES), lambda i: (lat(i), 0))
    return pl.pallas_call(
        functools.partial(_inproj_kernel, n_lat=n_lat),
        grid=(n_lat + 1,),
        in_specs=[pl.BlockSpec((tm, D_MODEL), lambda i: (lat(i), 0)), _const_spec((tm, D_MODEL)),
                  vec(D_MODEL), mod, mod, _const_spec((D_MODEL, HYB_IN)), vec(LANES), vec(LANES),
                  rtab, rtab, _const_spec((GRID_W, LANES)), _const_spec((GRID_W, LANES)),
                  _const_spec((2 * LANES, LANES)), _const_spec((F_W, 2 * F_W))],
        out_specs=[pl.BlockSpec((N_KV_HEADS, None, HEAD_DIM, GQA_GROUP * tm), lambda i: (0, lat(i), 0, 0)),
                   pl.BlockSpec((N_KV_HEADS, tm, HEAD_DIM), lambda i: (0, i, 0)),
                   pl.BlockSpec((N_KV_HEADS, V_ROWS, tm), lambda i: (0, 0, i)),
                   pl.BlockSpec((tm, 2 * F_W), lambda i: (lat(i), 0))],
        out_shape=[jax.ShapeDtypeStruct((N_KV_HEADS, n_lat, HEAD_DIM, GQA_GROUP * tm), BF16),
                   jax.ShapeDtypeStruct((N_KV_HEADS, lk, HEAD_DIM), BF16),
                   jax.ShapeDtypeStruct((N_KV_HEADS, V_ROWS, lk), BF16),
                   jax.ShapeDtypeStruct((n, 2 * F_W), BF16)],
        compiler_params=_cparams(1),
        name="inproj",
    )(x2d, ctx2d, gm, shift2, scale2, w_in, qg, kg, *rope, bd, dft)


def _rope_tables(n):
    inv_freq = 1.0 / (ROPE_THETA ** (np.arange(0, ROPE_HALF, 2, dtype=np.float64) / ROPE_HALF))
    rows = n // GRID_W
    e = np.arange(ROPE_HALF)
    sign = np.where(e < ROPE_HALF // 2, -1.0, 1.0)
    ang_r = np.arange(rows)[:, None] * inv_freq[e % (ROPE_HALF // 2)][None, :]
    ang_c = np.arange(GRID_W)[:, None] * inv_freq[e % (ROPE_HALF // 2)][None, :]

    def two_heads(row_half, col_half):
        t = np.concatenate([row_half, col_half], axis=1)
        return np.concatenate([t, t], axis=1)

    zr, zc = np.zeros_like(ang_r), np.zeros_like(ang_c)
    rep = lambda t: jnp.asarray(np.repeat(t, SUBLANES, axis=0), F32)
    return (rep(two_heads(np.cos(ang_r), zr)), rep(two_heads(np.sin(ang_r) * sign, zr)),
            jnp.asarray(two_heads(zc, np.cos(ang_c)), F32),
            jnp.asarray(two_heads(zc, np.sin(ang_c) * sign), F32))


def _headnorm_matrix():
    i = np.arange(LANES)
    m = (i[:, None] // HEAD_DIM == i[None, :] // HEAD_DIM) / HEAD_DIM
    return jnp.asarray(np.concatenate([m, m], axis=0), BF16)


def _channel_dft_matrix():
    i = np.arange(F_W)
    same = (i[:, None] // F_GROUP_DIM == i[None, :] // F_GROUP_DIM)
    ang = 2.0 * np.pi * ((i[:, None] % F_GROUP_DIM) * (i[None, :] % F_GROUP_DIM) % F_GROUP_DIM) / F_GROUP_DIM
    s = F_GROUP_DIM ** -0.5
    return jnp.asarray(np.concatenate([np.cos(ang) * same * s, np.sin(ang) * same * s], axis=1), BF16)


ATTN_TK = 256
ATTN_UNROLL = 65
ATTN_SLOT = tuple(s % 6 for s in range(ATTN_UNROLL))
ATTN_NSLOT = max(ATTN_SLOT) + 1
ATTN_UNSHIFTED_LIMIT = 80.0
ATTN_M_INIT = -1e30


def _attn_kernel(qt_ref, k_ref, vt_ref, o_ref, *scr, tk, unroll, running_max):
    s_scr = scr[:ATTN_NSLOT]
    p_scr = scr[ATTN_NSLOT:2 * ATTN_NSLOT]
    acc_scr = scr[2 * ATTN_NSLOT]
    if running_max:
        m_scr = scr[2 * ATTN_NSLOT + 1]
        a_scr = scr[2 * ATTN_NSLOT + 2:]
    slot_of = lambda s: ATTN_SLOT[s % unroll]
    nb = k_ref.shape[0] // tk
    qt = qt_ref[...]

    def scores(blk, slot):
        off = pl.multiple_of(blk * tk, tk)
        s_scr[slot][...] = jnp.dot(k_ref[pl.ds(off, tk), :], qt, preferred_element_type=F32)

    def probs(slot):
        s = s_scr[slot][...]
        if running_max:
            m_old = m_scr[...]
            m_new = jnp.maximum(m_old, jnp.max(s, axis=0, keepdims=True))
            a_scr[slot][...] = jnp.exp2(m_old - m_new)
            m_scr[...] = m_new
            s = s - m_new
        p_scr[slot][...] = jnp.exp2(s).astype(BF16)

    def accumulate(blk, slot):
        off = pl.multiple_of(blk * tk, tk)
        d = jnp.dot(vt_ref[:, pl.ds(off, tk)], p_scr[slot][...], preferred_element_type=F32)
        if running_max:
            acc_scr[...] = acc_scr[...] * a_scr[slot][...] + d
        else:
            acc_scr[...] += d

    def steps(j, n_probs, n_scores):
        for s in range(unroll):
            accumulate(j * unroll + s, slot_of(s))
            if s < n_probs:
                probs(slot_of(s + 1))
            if s < n_scores:
                scores(j * unroll + s + 2, slot_of(s + 2))

    def body(j, carry):
        steps(j, unroll, unroll)
        return carry

    acc_scr[...] = jnp.zeros_like(acc_scr)
    if running_max:
        m_scr[...] = jnp.full_like(m_scr, ATTN_M_INIT)
    scores(0, slot_of(0))
    scores(1, slot_of(1))
    probs(slot_of(0))
    n_iter = nb // unroll
    lax.fori_loop(0, n_iter - 1, body, 0)
    steps(n_iter - 1, unroll - 1, unroll - 2)
    acc = acc_scr[...]
    o = acc[:HEAD_DIM, :] * (1.0 / acc[HEAD_DIM:HEAD_DIM + 1, :])
    tq = o.shape[1] // GQA_GROUP
    outs = [o[:, g * tq:(g + 1) * tq].T for g in range(GQA_GROUP)]
    o_ref[...] = jnp.concatenate(outs, axis=1).astype(o_ref.dtype)


def _attention_call(qt, k, vt, *, running_max):
    tq, tk, unroll = ATTN_TQ, ATTN_TK, ATTN_UNROLL
    nq = qt.shape[1]
    lk = k.shape[1]
    assert lk % (tk * unroll) == 0 and unroll >= 3
    gw = GQA_GROUP * HEAD_DIM
    nlane = GQA_GROUP * tq
    scratch = ([pltpu.VMEM((tk, nlane), F32)] * ATTN_NSLOT
               + [pltpu.VMEM((tk, nlane), BF16)] * ATTN_NSLOT
               + [pltpu.VMEM((V_ROWS, nlane), F32)])
    if running_max:
        scratch += [pltpu.VMEM((1, nlane), F32)] * (1 + ATTN_NSLOT)
    return pl.pallas_call(
        functools.partial(_attn_kernel, tk=tk, unroll=unroll, running_max=running_max),
        grid=(N_KV_HEADS, nq),
        in_specs=[pl.BlockSpec((None, None, HEAD_DIM, nlane), lambda h, i: (h, i, 0, 0)),
                  pl.BlockSpec((None, lk, HEAD_DIM), lambda h, i: (h, 0, 0)),
                  pl.BlockSpec((None, V_ROWS, lk), lambda h, i: (h, 0, 0))],
        out_specs=pl.BlockSpec((tq, gw), lambda h, i: (i, h)),
        out_shape=jax.ShapeDtypeStruct((nq * tq, Q_W), BF16),
        scratch_shapes=scratch,
        compiler_params=_cparams(2),
        name="attention_running_max" if running_max else "attention",
    )(qt, k, vt)


def _attention(qt, k, vt, score_bound):
    return lax.cond(score_bound < ATTN_UNSHIFTED_LIMIT,
                    functools.partial(_attention_call, running_max=False),
                    functools.partial(_attention_call, running_max=True), qt, k, vt)


def _fft1_kernel(ab_ref, ca_ref, cb_ref, tr_ref, ti_ref, *, nblk):
    ca = ca_ref[...]
    cb = cb_ref[...]
    for j in range(nblk):
        a = ab_ref[:, j * 2 * F_W:j * 2 * F_W + F_W]
        b = ab_ref[:, j * 2 * F_W + F_W:(j + 1) * 2 * F_W]
        t = (jnp.dot(ca, a, preferred_element_type=F32)
             + jnp.dot(cb, b, preferred_element_type=F32))
        tr_ref[:, j * F_W:(j + 1) * F_W] = t[:FFT_N].astype(BF16)
        ti_ref[:, j * F_W:(j + 1) * F_W] = t[FFT_N:].astype(BF16)


def _fft2_kernel(tr_ref, ti_ref, m_ref, y_ref, *, nblk):
    for j in range(nblk):
        m = m_ref[j]
        y = (jnp.dot(m[:, :FFT_N], tr_ref[j * FFT_N:(j + 1) * FFT_N, :], preferred_element_type=F32)
             + jnp.dot(m[:, FFT_N:], ti_ref[j * FFT_N:(j + 1) * FFT_N, :], preferred_element_type=F32))
        y_ref[:, j * F_W:(j + 1) * F_W] = y.astype(BF16)


def _fft_tables():
    n = FFT_N
    l = n * n
    s = n ** -0.5
    k = np.arange(n)
    ang1 = 2.0 * np.pi * ((k[:, None] * k[None, :]) % n) / n
    c1, s1 = np.cos(ang1) * s, np.sin(ang1) * s
    ca = np.concatenate([c1, -s1], axis=0)
    cb = np.concatenate([-s1, -c1], axis=0)
    k1 = k[:, None, None]
    k2 = k[None, :, None]
    n2 = k[None, None, :]
    ang2 = 2.0 * np.pi * ((n2 * (k1 + n * k2)) % l) / l
    m = np.concatenate([np.cos(ang2) * s, np.sin(ang2) * s], axis=2)
    return jnp.asarray(ca, BF16), jnp.asarray(cb, BF16), jnp.asarray(m, BF16)


def _fourier(ab):
    l = ab.shape[0]
    assert l == FFT_N * FFT_N
    ca, cb, m = _fft_tables()
    nblk = 32
    ab_v = ab.reshape(FFT_N, FFT_N * 2 * F_W)
    tr, ti = pl.pallas_call(
        functools.partial(_fft1_kernel, nblk=nblk),
        grid=(FFT_N // nblk,),
        in_specs=[pl.BlockSpec((FFT_N, nblk * 2 * F_W), lambda i: (0, i)),
                  _const_spec((2 * FFT_N, FFT_N)), _const_spec((2 * FFT_N, FFT_N))],
        out_specs=[pl.BlockSpec((FFT_N, nblk * F_W), lambda i: (0, i))] * 2,
        out_shape=[jax.ShapeDtypeStruct((FFT_N, FFT_N * F_W), BF16)] * 2,
        compiler_params=_cparams(1),
        name="fft_stage1",
    )(ab_v, ca, cb)
    tr = tr.reshape(l, F_W)
    ti = ti.reshape(l, F_W)
    y = pl.pallas_call(
        functools.partial(_fft2_kernel, nblk=nblk),
        grid=(FFT_N // nblk,),
        in_specs=[pl.BlockSpec((nblk * FFT_N, F_W), lambda i: (i, 0)),
                  pl.BlockSpec((nblk * FFT_N, F_W), lambda i: (i, 0)),
                  pl.BlockSpec((nblk, FFT_N, 2 * FFT_N), lambda i: (i, 0, 0))],
        out_specs=pl.BlockSpec((FFT_N, nblk * F_W), lambda i: (0, i)),
        out_shape=jax.ShapeDtypeStruct((FFT_N, FFT_N * F_W), BF16),
        compiler_params=_cparams(1),
        name="fft_stage2",
    )(tr, ti, m)
    return y.reshape(l, F_W)


FFN_TM = 512
FFN_HALO = SUBLANES
FFN_CHUNKS = 1
BF16_ROWS = 2 * SUBLANES


def _ffn_tile(xp, x, xn, g_ref, sh_ref, sc_ref, gate_ref, wup_ref, wdw_ref, bdw_ref, wdn_ref):
    i = pl.program_id(0)
    last = pl.num_programs(0) - 1
    tm = x.shape[0]
    g, sh, sc = g_ref[...], sh_ref[...], sc_ref[...]
    hp = _rms_mod(xp, g, sh, sc) * jnp.where(i > 0, 1.0, 0.0)
    hn = _rms_mod(xn, g, sh, sc) * jnp.where(i < last, 1.0, 0.0)
    h = jnp.concatenate([hp, _rms_mod(x, g, sh, sc), hn], axis=0)
    rows = tm + 2 * FFN_HALO
    cw = FFN_DIM // FFN_CHUNKS

    def conv(u, c0):
        w = wdw_ref[:, c0:c0 + cw]
        um = pltpu.roll(u, 1, 0)[FFN_HALO:FFN_HALO + tm]
        up = pltpu.roll(u, rows - 1, 0)[FFN_HALO:FFN_HALO + tm]
        return (um * w[0:1] + u[FFN_HALO:FFN_HALO + tm] * w[1:2] + up * w[2:3]
                + bdw_ref[:, c0:c0 + cw])

    acc = None
    for c in range(FFN_CHUNKS):
        ca, cb = c * cw, FFN_DIM + c * cw
        a = conv(jnp.dot(h, wup_ref[:, ca:ca + cw], preferred_element_type=F32), ca)
        b = conv(jnp.dot(h, wup_ref[:, cb:cb + cw], preferred_element_type=F32), cb)
        act = _silu(a) * b
        d = jnp.dot(act, wdn_ref[ca:ca + cw, :], preferred_element_type=F32)
        acc = d if acc is None else acc + d
    return x + gate_ref[...] * acc


def _ffn_kernel(xp_ref, x_ref, xn_ref, *rest):
    *ffn_refs, o_ref = rest
    o_ref[...] = _ffn_tile(xp_ref[...], x_ref[...], xn_ref[...], *ffn_refs)


def _mix_ffn_kernel(xp_ref, x_ref, xn_ref, ap_ref, a_ref, an_ref, yp_ref, y_ref, yn_ref, wo_ref, g1_ref,
                    *rest):
    *ffn_refs, o_ref = rest
    tm = x_ref.shape[0]
    a_all = jnp.concatenate([a_ref[...], ap_ref[...], an_ref[...]], axis=0)
    y_all = jnp.concatenate([y_ref[...], yp_ref[...], yn_ref[...]], axis=0)
    o_all = (jnp.dot(a_all, wo_ref[:Q_W, :], preferred_element_type=F32)
             + jnp.dot(y_all, wo_ref[Q_W:, :], preferred_element_type=F32))
    g1 = g1_ref[...]
    prev_lo = tm + BF16_ROWS - FFN_HALO
    next_lo = tm + BF16_ROWS
    x1 = x_ref[...] + g1 * o_all[:tm]
    x1p = xp_ref[...] + g1 * o_all[prev_lo:prev_lo + FFN_HALO]
    x1n = xn_ref[...] + g1 * o_all[next_lo:next_lo + FFN_HALO]
    o_ref[...] = _ffn_tile(x1p, x1, x1n, *ffn_refs)


def _halo_specs(n, tm, halo, width):
    hb, nh = tm // halo, n // halo
    return [pl.BlockSpec((halo, width), lambda i: (jnp.maximum(i * hb - 1, 0), 0)),
            pl.BlockSpec((tm, width), lambda i: (i, 0)),
            pl.BlockSpec((halo, width), lambda i: (jnp.minimum((i + 1) * hb, nh - 1), 0))]


def _ffn_weight_specs(layer):
    vec = lambda: _const_spec((1, D_MODEL))
    per_layer = lambda r, c: pl.BlockSpec((None, r, c), lambda i: (layer, 0, 0),
                                          pipeline_mode=pl.Buffered(1))
    return [vec(), vec(), vec(), vec(),
            per_layer(D_MODEL, 2 * FFN_DIM), per_layer(3, 2 * FFN_DIM),
            per_layer(1, 2 * FFN_DIM), per_layer(FFN_DIM, D_MODEL)]


def _ffn(x2d, g, shift, scale, gate, layer, w_up, w_dw, b_dw, w_down):
    n = x2d.shape[0]
    tm = FFN_TM
    return pl.pallas_call(
        _ffn_kernel,
        grid=(n // tm,),
        in_specs=_halo_specs(n, tm, FFN_HALO, D_MODEL) + _ffn_weight_specs(layer),
        out_specs=pl.BlockSpec((tm, D_MODEL), lambda i: (i, 0)),
        out_shape=jax.ShapeDtypeStruct((n, D_MODEL), F32),
        compiler_params=_cparams(1),
        name="ffn",
    )(x2d, x2d, x2d, g, shift, scale, gate, w_up, w_dw, b_dw, w_down)


def _mix_ffn(x2d, att, fm, w_out, gate1, g, shift, scale, gate, layer, w_up, w_dw, b_dw, w_down):
    n = x2d.shape[0]
    tm = FFN_TM
    return pl.pallas_call(
        _mix_ffn_kernel,
        grid=(n // tm,),
        in_specs=(_halo_specs(n, tm, FFN_HALO, D_MODEL) + _halo_specs(n, tm, BF16_ROWS, Q_W)
                  + _halo_specs(n, tm, BF16_ROWS, F_W)
                  + [_const_spec((Q_W + F_W, D_MODEL)), _const_spec((1, D_MODEL))]
                  + _ffn_weight_specs(layer)),
        out_specs=pl.BlockSpec((tm, D_MODEL), lambda i: (i, 0)),
        out_shape=jax.ShapeDtypeStruct((n, D_MODEL), F32),
        compiler_params=_cparams(1),
        name="mix_ffn",
    )(x2d, x2d, x2d, att, att, att, fm, fm, fm, w_out, gate1, g, shift, scale, gate,
      w_up, w_dw, b_dw, w_down)


CONF_HALO = 2 * SUBLANES
CONF_TM = 512
CONF_N = CONF_TM + 2 * CONF_HALO
CONF_NF = 288


def _conv_dft_tables():
    n, nf = CONF_N, CONF_N // 2 + 1
    f = np.arange(CONF_NF)[:, None]
    live = f < nf
    t = np.arange(n)[None, :]
    ang = 2.0 * np.pi * ((f * t) % n) / n
    fwd = np.concatenate([np.cos(ang) * live, np.sin(ang) * live], axis=0)
    r = np.arange(CONF_HALO, CONF_HALO + CONF_TM)[:, None]
    fi = f.T
    weight = np.where((fi == 0) | (fi == n // 2), 1.0, 2.0) * live.T / n
    angi = 2.0 * np.pi * ((r * fi) % n) / n
    inv = np.concatenate([weight * np.cos(angi), -weight * np.sin(angi)], axis=1)
    k = np.arange(CONV_WIDTH)[None, :] - (CONV_WIDTH - 1) // 2
    angw = 2.0 * np.pi * ((f * k) % n) / n
    gw = np.concatenate([np.cos(angw) * live, np.sin(angw) * live], axis=0)
    return jnp.asarray(fwd, F32), jnp.asarray(inv, F32), jnp.asarray(gw, F32)


def _filter_spectrum_kernel(gw_ref, w_ref, o_ref):
    o_ref[...] = jnp.dot(gw_ref[...], w_ref[...], preferred_element_type=F32,
                         precision=lax.Precision.HIGHEST)


def _filter_spectrum(gw, wdw):
    return pl.pallas_call(
        _filter_spectrum_kernel,
        out_shape=jax.ShapeDtypeStruct((2 * CONF_NF, D_MODEL), F32),
        name="conv_filter_spectrum",
    )(gw, wdw)


def _conf_kernel(xp_ref, x_ref, xn_ref, g_ref, sh_ref, sc_ref, gate_ref, w1_ref, b1_ref, fwd_ref, inv_ref,
                 gs_ref, bdw_ref, lng_ref, lnb_ref, w2_ref, b2_ref, o_ref, *, seq_len):
    i = pl.program_id(0)
    tm = x_ref.shape[0]
    g, sh, sc = g_ref[...], sh_ref[...], sc_ref[...]
    x = x_ref[...]
    h = jnp.concatenate([_rms_mod(xp_ref[...], g, sh, sc), _rms_mod(x, g, sh, sc),
                         _rms_mod(xn_ref[...], g, sh, sc)], axis=0)
    u = jnp.dot(h, w1_ref[...], preferred_element_type=F32) + b1_ref[...]
    glu = u[:, :D_MODEL] * (1.0 / (1.0 + jnp.exp(-u[:, D_MODEL:])))
    pos = i * tm - CONF_HALO + lax.broadcasted_iota(jnp.int32, (CONF_N, 1), 0)
    glu = jnp.where((pos >= 0) & (pos < seq_len), glu, 0.0)
    spec = jnp.dot(fwd_ref[...], glu, preferred_element_type=F32)
    uc, us = spec[:CONF_NF], spec[CONF_NF:]
    gc, gs = gs_ref[:CONF_NF, :], gs_ref[CONF_NF:, :]
    y = jnp.concatenate([uc * gc + us * gs, uc * gs - us * gc], axis=0)
    acc = jnp.dot(inv_ref[...], y, preferred_element_type=F32) + bdw_ref[...]
    mu = jnp.mean(acc, axis=-1, keepdims=True)
    xc = acc - mu
    var = jnp.mean(xc * xc, axis=-1, keepdims=True)
    yn = xc * lax.rsqrt(var + LN_EPS) * lng_ref[...] + lnb_ref[...]
    o = jnp.dot(_silu(yn), w2_ref[...], preferred_element_type=F32) + b2_ref[...]
    o_ref[...] = x + gate_ref[...] * o


def _conformer(x2d, g, shift, scale, gate, w1, b1, wdw, bdw, lng, lnb, w2, b2):
    n = x2d.shape[0]
    tm = CONF_TM
    hb = tm // CONF_HALO
    nh = n // CONF_HALO
    fwd, inv, gw = _conv_dft_tables()
    gspec = _filter_spectrum(gw, wdw)
    vec = lambda w=D_MODEL: _const_spec((1, w))
    return pl.pallas_call(
        functools.partial(_conf_kernel, seq_len=n),
        grid=(n // tm,),
        in_specs=[pl.BlockSpec((CONF_HALO, D_MODEL), lambda i: (jnp.maximum(i * hb - 1, 0), 0)),
                  pl.BlockSpec((tm, D_MODEL), lambda i: (i, 0)),
                  pl.BlockSpec((CONF_HALO, D_MODEL), lambda i: (jnp.minimum((i + 1) * hb, nh - 1), 0)),
                  vec(), vec(), vec(), vec(),
                  _const_spec((D_MODEL, 2 * D_MODEL)), vec(2 * D_MODEL),
                  _const_spec((2 * CONF_NF, CONF_N)), _const_spec((tm, 2 * CONF_NF)),
                  _const_spec((2 * CONF_NF, D_MODEL)), vec(), vec(), vec(),
                  _const_spec((D_MODEL, D_MODEL)), vec()],
        out_specs=pl.BlockSpec((tm, D_MODEL), lambda i: (i, 0)),
        out_shape=jax.ShapeDtypeStruct((n, D_MODEL), F32),
        compiler_params=_cparams(1),
        name="conformer",
    )(x2d, x2d, x2d, g, shift, scale, gate, w1, b1, fwd, inv, gspec, bdw, lng, lnb, w2, b2)


def kernel(x, c, ctx, c_ctx, w_ada, b_ada, g_mix, g_ffn, w_in_hyb, q_gain, k_gain, w_out_hyb,
           w_pw1, b_pw1, w_cdw, b_cdw, ln_g, ln_b, w_pw2, b_pw2, w_up, w_fdw, b_fdw, w_down):
    batch, seq, d = x.shape
    assert batch == 1 and d == D_MODEL
    x2d = x.reshape(seq, d)
    ctx2d = ctx.reshape(-1, d)
    row = lambda v: v.reshape(1, -1)

    cond = jnp.zeros((SUBLANES, d), F32).at[0].set(c[0]).at[1].set(c_ctx)
    mods = _ada(cond, w_ada, b_ada)
    mod = lambda layer, who, j: mods[layer, who:who + 1, j * d:(j + 1) * d]

    w_in = w_in_hyb[0]
    qg = row(jnp.tile(q_gain[0], LANES // HEAD_DIM))
    kg = row(jnp.tile(k_gain[0], LANES // HEAD_DIM))
    bd = _headnorm_matrix()
    dft = _channel_dft_matrix()
    shift2 = mods[0, 0:2, 0:d].reshape(2, 1, d)
    scale2 = mods[0, 0:2, d:2 * d].reshape(2, 1, d)
    qt, k_h, vt, ab = _inproj(x2d, ctx2d, row(g_mix[0]), shift2, scale2, w_in, qg, kg,
                              _rope_tables(seq), bd, dft)
    score_bound = HEAD_DIM * Q_SCALE * jnp.max(jnp.abs(q_gain[0])) * jnp.max(jnp.abs(k_gain[0]))
    att = _attention(qt, k_h, vt, score_bound)
    fm = _fourier(ab)
    ffn_w = (w_up, w_fdw, b_fdw.reshape(b_fdw.shape[0], 1, -1), w_down)
    x2 = _mix_ffn(x2d, att, fm, w_out_hyb[0].astype(BF16), mod(0, 0, 2),
                  row(g_ffn[0]), mod(0, 0, 3), mod(0, 0, 4), mod(0, 0, 5), 0, *ffn_w)

    x3 = _conformer(x2, row(g_mix[1]), mod(1, 0, 0), mod(1, 0, 1), mod(1, 0, 2),
                    w_pw1[0], row(b_pw1[0]), w_cdw[0], row(b_cdw[0]),
                    row(ln_g[0]), row(ln_b[0]), w_pw2[0], row(b_pw2[0]))
    x4 = _ffn(x3, row(g_ffn[1]), mod(1, 0, 3), mod(1, 0, 4), mod(1, 0, 5), 1, *ffn_w)
    return x4.reshape(batch, seq, d)
```

```python
import functools
import math

import numpy as np
import jax
import jax.numpy as jnp
from jax import lax
from jax.experimental import pallas as pl
from jax.experimental.pallas import tpu as pltpu

F32 = jnp.float32
BF16 = jnp.bfloat16

D_MODEL = 1024
GRID_W = 64
HEAD_DIM = 64
N_Q_HEADS = 8
N_KV_HEADS = 2
GQA_GROUP = N_Q_HEADS // N_KV_HEADS
Q_W = N_Q_HEADS * HEAD_DIM
KV_W = N_KV_HEADS * HEAD_DIM
F_GROUPS = 8
F_GROUP_DIM = 64
F_W = F_GROUPS * F_GROUP_DIM
HYB_IN = Q_W + 2 * KV_W + F_W
ROPE_HALF = HEAD_DIM // 2
ROPE_THETA = 10000.0
CONV_WIDTH = 31
FFN_DIM = 2816
NORM_EPS = 1e-6
LN_EPS = 1e-5

LANES = 128
SUBLANES = 8
FFT_N = 128
FFT_BLOCKS = 32
ADA_TN = 1536
V_ROWS = 80
VMEM_LIMIT = 60 * 1024 * 1024

Q_SCALE = HEAD_DIM ** -0.5 * math.log2(math.e)


def _cparams(n_axes=1):
    return pltpu.CompilerParams(dimension_semantics=("arbitrary",) * n_axes,
                                vmem_limit_bytes=VMEM_LIMIT)


def _const_spec(shape):
    zeros = (0,) * len(shape)
    return pl.BlockSpec(shape, lambda *_: zeros, pipeline_mode=pl.Buffered(1))


def _rms_mod(x, g, shift, scale):
    ms = jnp.mean(x * x, axis=-1, keepdims=True)
    return (x * lax.rsqrt(ms + NORM_EPS) * g) * (1.0 + scale) + shift


def _silu(x):
    return x * (1.0 / (1.0 + jnp.exp(-x)))


def _dot_3pass(a, b):
    def split(v):
        hi = v.astype(BF16)
        return hi, (v - hi.astype(F32)).astype(BF16)

    a_hi, a_lo = split(a)
    b_hi, b_lo = split(b)
    dot = functools.partial(jnp.dot, preferred_element_type=F32)
    return dot(a_hi, b_hi) + dot(a_lo, b_hi) + dot(a_hi, b_lo)


def _ada_kernel(cond_ref, w_ref, b_ref, o_ref):
    o_ref[0] = _dot_3pass(_silu(cond_ref[...]), w_ref[0]) + b_ref[0]


def _ada(cond, w_ada, b_ada):
    depth, d, n = w_ada.shape
    tn = ADA_TN
    return pl.pallas_call(
        _ada_kernel,
        grid=(depth, n // tn),
        in_specs=[pl.BlockSpec((SUBLANES, d), lambda i, j: (0, 0)),
                  pl.BlockSpec((1, d, tn), lambda i, j: (i, 0, j)),
                  pl.BlockSpec((1, 1, tn), lambda i, j: (i, 0, j))],
        out_specs=pl.BlockSpec((1, SUBLANES, tn), lambda i, j: (i, 0, j)),
        out_shape=jax.ShapeDtypeStruct((depth, SUBLANES, n), F32),
        compiler_params=_cparams(2),
        name="ada",
    )(cond, w_ada, b_ada.reshape(depth, 1, n))


INPROJ_TM = ATTN_TQ = 256
GRID_ROWS_PER_TILE = INPROJ_TM // GRID_W


def _inproj_kernel(x_ref, ctx_ref, gm_ref, sh_ref, sc_ref, w_ref, qg_ref, kg_ref, rcos_ref, rsin_ref,
                   ccos_ref, csin_ref, bd_ref, dft_ref, qt_ref, k_ref, vt_ref, ab_ref, *, n_lat):
    i = pl.program_id(0)
    is_ctx = i >= n_lat
    xin = jnp.where(is_ctx, ctx_ref[...], x_ref[...])
    h = _rms_mod(xin, gm_ref[...], sh_ref[...], sc_ref[...])
    u = jnp.dot(h, w_ref[...], preferred_element_type=F32)

    def rope_table(row_ref, col_ref, ctx_value):
        rows = [jnp.tile(row_ref[g * SUBLANES:(g + 1) * SUBLANES, :], (GRID_W // SUBLANES, 1))
                for g in range(GRID_ROWS_PER_TILE)]
        t = jnp.concatenate(rows, axis=0) + jnp.tile(col_ref[...], (GRID_ROWS_PER_TILE, 1))
        return jnp.where(is_ctx, ctx_value, t)

    cos = rope_table(rcos_ref, ccos_ref, 1.0)
    sin = rope_table(rsin_ref, csin_ref, 0.0)
    bd = bd_ref[...]
    lane = lax.broadcasted_iota(jnp.int32, cos.shape, 1)
    first = (lane % ROPE_HALF) < (ROPE_HALF // 2)

    def head_norm_rope(t, gain):
        sq = t * t
        hi = sq.astype(BF16)
        lo = (sq - hi.astype(F32)).astype(BF16)
        ms = jnp.dot(jnp.concatenate([hi, lo], axis=1), bd, preferred_element_type=F32)
        tn = t * lax.rsqrt(ms + NORM_EPS) * gain
        partner = jnp.where(first, pltpu.roll(tn, LANES - ROPE_HALF // 2, 1),
                            pltpu.roll(tn, ROPE_HALF // 2, 1))
        return tn * cos + partner * sin

    kk = head_norm_rope(u[:, Q_W:Q_W + KV_W], kg_ref[...]).astype(BF16)
    vt = u[:, Q_W + KV_W:Q_W + 2 * KV_W].T
    pad_rows = lax.broadcasted_iota(jnp.int32, (V_ROWS - HEAD_DIM, INPROJ_TM), 0)
    ones_rows = jnp.where(pad_rows == 0, 1.0, 0.0).astype(BF16)
    for kv in range(N_KV_HEADS):
        k_ref[kv] = kk[:, kv * HEAD_DIM:(kv + 1) * HEAD_DIM]
        vt_ref[kv, :HEAD_DIM, :] = vt[kv * HEAD_DIM:(kv + 1) * HEAD_DIM, :].astype(BF16)
        vt_ref[kv, HEAD_DIM:, :] = ones_rows

    @pl.when(i < n_lat)
    def _():
        qg = qg_ref[...]
        for j in range(Q_W // LANES):
            t = head_norm_rope(u[:, j * LANES:(j + 1) * LANES], qg) * Q_SCALE
            tt = t.astype(BF16).T
            for hh in range(LANES // HEAD_DIM):
                head = j * (LANES // HEAD_DIM) + hh
                kv, g = head // GQA_GROUP, head % GQA_GROUP
                qt_ref[kv, :, g * INPROJ_TM:(g + 1) * INPROJ_TM] = tt[hh * HEAD_DIM:(hh + 1) * HEAD_DIM, :]
        f = u[:, Q_W + 2 * KV_W:].astype(BF16)
        ab_ref[...] = jnp.dot(f, dft_ref[...], preferred_element_type=F32).astype(BF16)


def _inproj(x2d, ctx2d, gm, shift2, scale2, w_in, qg, kg, rope, bd, dft):
    n = x2d.shape[0]
    tm = INPROJ_TM
    n_lat = n // tm
    assert ctx2d.shape[0] == tm and n % tm == 0
    lk = n + tm
    lat = lambda i: jnp.minimum(i, n_lat - 1)
    vec = lambda w: _const_spec((1, w))
    mod = pl.BlockSpec((None, 1, D_MODEL), lambda i: (i // n_lat, 0, 0))
    rtab = pl.BlockSpec((GRID_ROWS_PER_TILE * SUBLANES, LANES), lambda i: (lat(i), 0))
    return pl.pallas_call(
        functools.partial(_inproj_kernel, n_lat=n_lat),
        grid=(n_lat + 1,),
        in_specs=[pl.BlockSpec((tm, D_MODEL), lambda i: (lat(i), 0)), _const_spec((tm, D_MODEL)),
                  vec(D_MODEL), mod, mod, _const_spec((D_MODEL, HYB_IN)), vec(LANES), vec(LANES),
                  rtab, rtab, _const_spec((GRID_W, LANES)), _const_spec((GRID_W, LANES)),
                  _const_spec((2 * LANES, LANES)), _const_spec((F_W, 2 * F_W))],
        out_specs=[pl.BlockSpec((N_KV_HEADS, None, HEAD_DIM, GQA_GROUP * tm), lambda i: (0, lat(i), 0, 0)),
                   pl.BlockSpec((N_KV_HEADS, tm, HEAD_DIM), lambda i: (0, i, 0)),
                   pl.BlockSpec((N_KV_HEADS, V_ROWS, tm), lambda i: (0, 0, i)),
                   pl.BlockSpec((tm, 2 * F_W), lambda i: (lat(i), 0))],
        out_shape=[jax.ShapeDtypeStruct((N_KV_HEADS, n_lat, HEAD_DIM, GQA_GROUP * tm), BF16),
                   jax.ShapeDtypeStruct((N_KV_HEADS, lk, HEAD_DIM), BF16),
                   jax.ShapeDtypeStruct((N_KV_HEADS, V_ROWS, lk), BF16),
                   jax.ShapeDtypeStruct((n, 2 * F_W), BF16)],
        compiler_params=_cparams(1),
        name="inproj",
    )(x2d, ctx2d, gm, shift2, scale2, w_in, qg, kg, *rope, bd, dft)


def _rope_tables(n):
    inv_freq = 1.0 / (ROPE_THETA ** (np.arange(0, ROPE_HALF, 2, dtype=np.float64) / ROPE_HALF))
    rows = n // GRID_W
    e = np.arange(ROPE_HALF)
    sign = np.where(e < ROPE_HALF // 2, -1.0, 1.0)
    ang_r = np.arange(rows)[:, None] * inv_freq[e % (ROPE_HALF // 2)][None, :]
    ang_c = np.arange(GRID_W)[:, None] * inv_freq[e % (ROPE_HALF // 2)][None, :]

    def two_heads(row_half, col_half):
        t = np.concatenate([row_half, col_half], axis=1)
        return np.concatenate([t, t], axis=1)

    zr, zc = np.zeros_like(ang_r), np.zeros_like(ang_c)
    rep = lambda t: jnp.asarray(np.repeat(t, SUBLANES, axis=0), F32)
    return (rep(two_heads(np.cos(ang_r), zr)), rep(two_heads(np.sin(ang_r) * sign, zr)),
            jnp.asarray(two_heads(zc, np.cos(ang_c)), F32),
            jnp.asarray(two_heads(zc, np.sin(ang_c) * sign), F32))


def _headnorm_matrix():
    i = np.arange(LANES)
    m = (i[:, None] // HEAD_DIM == i[None, :] // HEAD_DIM) / HEAD_DIM
    return jnp.asarray(np.concatenate([m, m], axis=0), BF16)


def _channel_dft_matrix():
    i = np.arange(F_W)
    same = (i[:, None] // F_GROUP_DIM == i[None, :] // F_GROUP_DIM)
    ang = 2.0 * np.pi * ((i[:, None] % F_GROUP_DIM) * (i[None, :] % F_GROUP_DIM) % F_GROUP_DIM) / F_GROUP_DIM
    s = F_GROUP_DIM ** -0.5
    return jnp.asarray(np.concatenate([np.cos(ang) * same * s, np.sin(ang) * same * s], axis=1), BF16)


ATTN_TK = 256
ATTN_UNROLL = 65
ATTN_SLOT = tuple(s % 6 for s in range(ATTN_UNROLL))
ATTN_NSLOT = max(ATTN_SLOT) + 1
ATTN_UNSHIFTED_LIMIT = 80.0
ATTN_M_INIT = -1e30


def _attn_kernel(qt_ref, k_ref, vt_ref, o_ref, *scr, tk, unroll, running_max):
    s_scr = scr[:ATTN_NSLOT]
    p_scr = scr[ATTN_NSLOT:2 * ATTN_NSLOT]
    acc_scr = scr[2 * ATTN_NSLOT]
    if running_max:
        m_scr = scr[2 * ATTN_NSLOT + 1]
        a_scr = scr[2 * ATTN_NSLOT + 2:]
    slot_of = lambda s: ATTN_SLOT[s % unroll]
    nb = k_ref.shape[0] // tk
    qt = qt_ref[...]

    def scores(blk, slot):
        off = pl.multiple_of(blk * tk, tk)
        s_scr[slot][...] = jnp.dot(k_ref[pl.ds(off, tk), :], qt, preferred_element_type=F32)

    def probs(slot):
        s = s_scr[slot][...]
        if running_max:
            m_old = m_scr[...]
            m_new = jnp.maximum(m_old, jnp.max(s, axis=0, keepdims=True))
            a_scr[slot][...] = jnp.exp2(m_old - m_new)
            m_scr[...] = m_new
            s = s - m_new
        p_scr[slot][...] = jnp.exp2(s).astype(BF16)

    def accumulate(blk, slot):
        off = pl.multiple_of(blk * tk, tk)
        d = jnp.dot(vt_ref[:, pl.ds(off, tk)], p_scr[slot][...], preferred_element_type=F32)
        if running_max:
            acc_scr[...] = acc_scr[...] * a_scr[slot][...] + d
        else:
            acc_scr[...] += d

    def steps(j, n_probs, n_scores):
        for s in range(unroll):
            accumulate(j * unroll + s, slot_of(s))
            if s < n_probs:
                probs(slot_of(s + 1))
            if s < n_scores:
                scores(j * unroll + s + 2, slot_of(s + 2))

    def body(j, carry):
        steps(j, unroll, unroll)
        return carry

    acc_scr[...] = jnp.zeros_like(acc_scr)
    if running_max:
        m_scr[...] = jnp.full_like(m_scr, ATTN_M_INIT)
    scores(0, slot_of(0))
    scores(1, slot_of(1))
    probs(slot_of(0))
    n_iter = nb // unroll
    lax.fori_loop(0, n_iter - 1, body, 0)
    steps(n_iter - 1, unroll - 1, unroll - 2)
    acc = acc_scr[...]
    o = acc[:HEAD_DIM, :] * (1.0 / acc[HEAD_DIM:HEAD_DIM + 1, :])
    tq = o.shape[1] // GQA_GROUP
    outs = [o[:, g * tq:(g + 1) * tq].T for g in range(GQA_GROUP)]
    o_ref[...] = jnp.concatenate(outs, axis=1).astype(o_ref.dtype)


def _attention_call(qt, k, vt, *, running_max):
    tq, tk, unroll = ATTN_TQ, ATTN_TK, ATTN_UNROLL
    nq = qt.shape[1]
    lk = k.shape[1]
    assert lk % (tk * unroll) == 0 and unroll >= 3
    gw = GQA_GROUP * HEAD_DIM
    nlane = GQA_GROUP * tq
    scratch = ([pltpu.VMEM((tk, nlane), F32)] * ATTN_NSLOT
               + [pltpu.VMEM((tk, nlane), BF16)] * ATTN_NSLOT
               + [pltpu.VMEM((V_ROWS, nlane), F32)])
    if running_max:
        scratch += [pltpu.VMEM((1, nlane), F32)] * (1 + ATTN_NSLOT)
    return pl.pallas_call(
        functools.partial(_attn_kernel, tk=tk, unroll=unroll, running_max=running_max),
        grid=(N_KV_HEADS, nq),
        in_specs=[pl.BlockSpec((None, None, HEAD_DIM, nlane), lambda h, i: (h, i, 0, 0)),
                  pl.BlockSpec((None, lk, HEAD_DIM), lambda h, i: (h, 0, 0)),
                  pl.BlockSpec((None, V_ROWS, lk), lambda h, i: (h, 0, 0))],
        out_specs=pl.BlockSpec((tq, gw), lambda h, i: (i, h)),
        out_shape=jax.ShapeDtypeStruct((nq * tq, Q_W), BF16),
        scratch_shapes=scratch,
        compiler_params=_cparams(2),
        name="attention_running_max" if running_max else "attention",
    )(qt, k, vt)


def _attention(qt, k, vt, score_bound):
    return lax.cond(score_bound < ATTN_UNSHIFTED_LIMIT,
                    functools.partial(_attention_call, running_max=False),
                    functools.partial(_attention_call, running_max=True), qt, k, vt)


def _fft1_kernel(ab_ref, ca_ref, cb_ref, t_ref, *, nblk):
    ca = ca_ref[...]
    cb = cb_ref[...]
    for j in range(nblk):
        a = ab_ref[:, j * 2 * F_W:j * 2 * F_W + F_W]
        b = ab_ref[:, j * 2 * F_W + F_W:(j + 1) * 2 * F_W]
        t = (jnp.dot(ca, a, preferred_element_type=F32)
             + jnp.dot(cb, b, preferred_element_type=F32))
        t_ref[:, j * 2 * F_W:j * 2 * F_W + F_W] = t[:FFT_N].astype(BF16)
        t_ref[:, j * 2 * F_W + F_W:(j + 1) * 2 * F_W] = t[FFT_N:].astype(BF16)


def _fft2_kernel(t_ref, m_ref, y_ref, *, nblk):
    for j in range(nblk):
        m = m_ref[j]
        rows = slice(j * FFT_N, (j + 1) * FFT_N)
        y = (jnp.dot(m[:, :FFT_N], t_ref[rows, :F_W], preferred_element_type=F32)
             + jnp.dot(m[:, FFT_N:], t_ref[rows, F_W:], preferred_element_type=F32))
        y_ref[:, j * F_W:(j + 1) * F_W] = y.astype(BF16)


def _fft_tables():
    n = FFT_N
    l = n * n
    s = n ** -0.5
    k = np.arange(n)
    ang1 = 2.0 * np.pi * ((k[:, None] * k[None, :]) % n) / n
    c1, s1 = np.cos(ang1) * s, np.sin(ang1) * s
    ca = np.concatenate([c1, -s1], axis=0)
    cb = np.concatenate([-s1, -c1], axis=0)
    k1 = k[:, None, None]
    k2 = k[None, :, None]
    n2 = k[None, None, :]
    ang2 = 2.0 * np.pi * ((n2 * (k1 + n * k2)) % l) / l
    m = np.concatenate([np.cos(ang2) * s, np.sin(ang2) * s], axis=2)
    return jnp.asarray(ca, BF16), jnp.asarray(cb, BF16), jnp.asarray(m, BF16)


def _fourier(ab):
    l = ab.shape[0]
    assert l == FFT_N * FFT_N
    ca, cb, m = _fft_tables()
    nblk = FFT_BLOCKS
    ab_v = ab.reshape(FFT_N, FFT_N * 2 * F_W)
    t = pl.pallas_call(
        functools.partial(_fft1_kernel, nblk=nblk),
        grid=(FFT_N // nblk,),
        in_specs=[pl.BlockSpec((FFT_N, nblk * 2 * F_W), lambda i: (0, i)),
                  _const_spec((2 * FFT_N, FFT_N)), _const_spec((2 * FFT_N, FFT_N))],
        out_specs=pl.BlockSpec((FFT_N, nblk * 2 * F_W), lambda i: (0, i)),
        out_shape=jax.ShapeDtypeStruct((FFT_N, FFT_N * 2 * F_W), BF16),
        compiler_params=_cparams(1),
        name="fft_stage1",
    )(ab_v, ca, cb)
    y = pl.pallas_call(
        functools.partial(_fft2_kernel, nblk=nblk),
        grid=(FFT_N // nblk,),
        in_specs=[pl.BlockSpec((nblk * FFT_N, 2 * F_W), lambda i: (i, 0)),
                  pl.BlockSpec((nblk, FFT_N, 2 * FFT_N), lambda i: (i, 0, 0))],
        out_specs=pl.BlockSpec((FFT_N, nblk * F_W), lambda i: (0, i)),
        out_shape=jax.ShapeDtypeStruct((FFT_N, FFT_N * F_W), BF16),
        compiler_params=_cparams(1),
        name="fft_stage2",
    )(t.reshape(l, 2 * F_W), m)
    return y.reshape(l, F_W)


FFN_TM = 512
FFN_HALO = SUBLANES
FFN_CHUNKS = 1
BF16_ROWS = 2 * SUBLANES


def _ffn_tile(xp, x, xn, g_ref, sh_ref, sc_ref, gate_ref, wup_ref, wdw_ref, bdw_ref, wdn_ref):
    i = pl.program_id(0)
    last = pl.num_programs(0) - 1
    tm = x.shape[0]
    g, sh, sc = g_ref[...], sh_ref[...], sc_ref[...]
    hp = _rms_mod(xp, g, sh, sc) * jnp.where(i > 0, 1.0, 0.0)
    hn = _rms_mod(xn, g, sh, sc) * jnp.where(i < last, 1.0, 0.0)
    h = jnp.concatenate([hp, _rms_mod(x, g, sh, sc), hn], axis=0)
    rows = tm + 2 * FFN_HALO
    cw = FFN_DIM // FFN_CHUNKS

    def conv(u, c0):
        w = wdw_ref[:, c0:c0 + cw]
        um = pltpu.roll(u, 1, 0)[FFN_HALO:FFN_HALO + tm]
        up = pltpu.roll(u, rows - 1, 0)[FFN_HALO:FFN_HALO + tm]
        return (um * w[0:1] + u[FFN_HALO:FFN_HALO + tm] * w[1:2] + up * w[2:3]
                + bdw_ref[:, c0:c0 + cw])

    acc = None
    for c in range(FFN_CHUNKS):
        ca, cb = c * cw, FFN_DIM + c * cw
        a = conv(jnp.dot(h, wup_ref[:, ca:ca + cw], preferred_element_type=F32), ca)
        b = conv(jnp.dot(h, wup_ref[:, cb:cb + cw], preferred_element_type=F32), cb)
        act = _silu(a) * b
        d = jnp.dot(act, wdn_ref[ca:ca + cw, :], preferred_element_type=F32)
        acc = d if acc is None else acc + d
    return x + gate_ref[...] * acc


def _ffn_kernel(xp_ref, x_ref, xn_ref, *rest):
    *ffn_refs, o_ref = rest
    o_ref[...] = _ffn_tile(xp_ref[...], x_ref[...], xn_ref[...], *ffn_refs)


def _mix_ffn_kernel(xp_ref, x_ref, xn_ref, ap_ref, a_ref, an_ref, yp_ref, y_ref, yn_ref, wo_ref, g1_ref,
                    *rest):
    *ffn_refs, o_ref = rest
    tm = x_ref.shape[0]
    a_all = jnp.concatenate([a_ref[...], ap_ref[...], an_ref[...]], axis=0)
    y_all = jnp.concatenate([y_ref[...], yp_ref[...], yn_ref[...]], axis=0)
    o_all = (jnp.dot(a_all, wo_ref[:Q_W, :], preferred_element_type=F32)
             + jnp.dot(y_all, wo_ref[Q_W:, :], preferred_element_type=F32))
    g1 = g1_ref[...]
    prev_lo = tm + BF16_ROWS - FFN_HALO
    next_lo = tm + BF16_ROWS
    x1 = x_ref[...] + g1 * o_all[:tm]
    x1p = xp_ref[...] + g1 * o_all[prev_lo:prev_lo + FFN_HALO]
    x1n = xn_ref[...] + g1 * o_all[next_lo:next_lo + FFN_HALO]
    o_ref[...] = _ffn_tile(x1p, x1, x1n, *ffn_refs)


def _halo_specs(n, tm, halo, width):
    hb, nh = tm // halo, n // halo
    return [pl.BlockSpec((halo, width), lambda i: (jnp.maximum(i * hb - 1, 0), 0)),
            pl.BlockSpec((tm, width), lambda i: (i, 0)),
            pl.BlockSpec((halo, width), lambda i: (jnp.minimum((i + 1) * hb, nh - 1), 0))]


def _ffn_weight_specs(layer):
    vec = lambda: _const_spec((1, D_MODEL))
    per_layer = lambda r, c: pl.BlockSpec((None, r, c), lambda i: (layer, 0, 0),
                                          pipeline_mode=pl.Buffered(1))
    return [vec(), vec(), vec(), vec(),
            per_layer(D_MODEL, 2 * FFN_DIM), per_layer(3, 2 * FFN_DIM),
            per_layer(1, 2 * FFN_DIM), per_layer(FFN_DIM, D_MODEL)]


def _ffn(x2d, g, shift, scale, gate, layer, w_up, w_dw, b_dw, w_down):
    n = x2d.shape[0]
    tm = FFN_TM
    return pl.pallas_call(
        _ffn_kernel,
        grid=(n // tm,),
        in_specs=_halo_specs(n, tm, FFN_HALO, D_MODEL) + _ffn_weight_specs(layer),
        out_specs=pl.BlockSpec((tm, D_MODEL), lambda i: (i, 0)),
        out_shape=jax.ShapeDtypeStruct((n, D_MODEL), F32),
        compiler_params=_cparams(1),
        name="ffn",
    )(x2d, x2d, x2d, g, shift, scale, gate, w_up, w_dw, b_dw, w_down)


def _mix_ffn(x2d, att, fm, w_out, gate1, g, shift, scale, gate, layer, w_up, w_dw, b_dw, w_down):
    n = x2d.shape[0]
    tm = FFN_TM
    return pl.pallas_call(
        _mix_ffn_kernel,
        grid=(n // tm,),
        in_specs=(_halo_specs(n, tm, FFN_HALO, D_MODEL) + _halo_specs(n, tm, BF16_ROWS, Q_W)
                  + _halo_specs(n, tm, BF16_ROWS, F_W)
                  + [_const_spec((Q_W + F_W, D_MODEL)), _const_spec((1, D_MODEL))]
                  + _ffn_weight_specs(layer)),
        out_specs=pl.BlockSpec((tm, D_MODEL), lambda i: (i, 0)),
        out_shape=jax.ShapeDtypeStruct((n, D_MODEL), F32),
        compiler_params=_cparams(1),
        name="mix_ffn",
    )(x2d, x2d, x2d, att, att, att, fm, fm, fm, w_out, gate1, g, shift, scale, gate,
      w_up, w_dw, b_dw, w_down)


CONF_HALO = 2 * SUBLANES
CONF_TM = 512
CONF_N = CONF_TM + 2 * CONF_HALO
CONF_NF = 288


def _conv_dft_tables():
    n, nf = CONF_N, CONF_N // 2 + 1
    f = np.arange(CONF_NF)[:, None]
    live = f < nf
    t = np.arange(n)[None, :]
    ang = 2.0 * np.pi * ((f * t) % n) / n
    fwd = np.concatenate([np.cos(ang) * live, np.sin(ang) * live], axis=0)
    r = np.arange(CONF_HALO, CONF_HALO + CONF_TM)[:, None]
    fi = f.T
    weight = np.where((fi == 0) | (fi == n // 2), 1.0, 2.0) * live.T / n
    angi = 2.0 * np.pi * ((r * fi) % n) / n
    inv = np.concatenate([weight * np.cos(angi), -weight * np.sin(angi)], axis=1)
    k = np.arange(CONV_WIDTH)[None, :] - (CONV_WIDTH - 1) // 2
    angw = 2.0 * np.pi * ((f * k) % n) / n
    gw = np.concatenate([np.cos(angw) * live, np.sin(angw) * live], axis=0)
    return jnp.asarray(fwd, F32), jnp.asarray(inv, F32), jnp.asarray(gw, F32)


def _filter_spectrum_kernel(gw_ref, w_ref, o_ref):
    o_ref[...] = _dot_3pass(gw_ref[...], w_ref[...])


def _filter_spectrum(gw, wdw):
    return pl.pallas_call(
        _filter_spectrum_kernel,
        out_shape=jax.ShapeDtypeStruct((2 * CONF_NF, D_MODEL), F32),
        name="conv_filter_spectrum",
    )(gw, wdw)


def _conf_kernel(xp_ref, x_ref, xn_ref, g_ref, sh_ref, sc_ref, gate_ref, w1_ref, b1_ref, fwd_ref, inv_ref,
                 gs_ref, bdw_ref, lng_ref, lnb_ref, w2_ref, b2_ref, o_ref, *, seq_len):
    i = pl.program_id(0)
    tm = x_ref.shape[0]
    g, sh, sc = g_ref[...], sh_ref[...], sc_ref[...]
    x = x_ref[...]
    h = jnp.concatenate([_rms_mod(xp_ref[...], g, sh, sc), _rms_mod(x, g, sh, sc),
                         _rms_mod(xn_ref[...], g, sh, sc)], axis=0)
    u = jnp.dot(h, w1_ref[...], preferred_element_type=F32) + b1_ref[...]
    glu = u[:, :D_MODEL] * (1.0 / (1.0 + jnp.exp(-u[:, D_MODEL:])))
    pos = i * tm - CONF_HALO + lax.broadcasted_iota(jnp.int32, (CONF_N, 1), 0)
    glu = jnp.where((pos >= 0) & (pos < seq_len), glu, 0.0)
    spec = jnp.dot(fwd_ref[...], glu, preferred_element_type=F32)
    uc, us = spec[:CONF_NF], spec[CONF_NF:]
    gc, gs = gs_ref[:CONF_NF, :], gs_ref[CONF_NF:, :]
    y = jnp.concatenate([uc * gc + us * gs, uc * gs - us * gc], axis=0)
    acc = jnp.dot(inv_ref[...], y, preferred_element_type=F32) + bdw_ref[...]
    mu = jnp.mean(acc, axis=-1, keepdims=True)
    xc = acc - mu
    var = jnp.mean(xc * xc, axis=-1, keepdims=True)
    yn = xc * lax.rsqrt(var + LN_EPS) * lng_ref[...] + lnb_ref[...]
    o = jnp.dot(_silu(yn), w2_ref[...], preferred_element_type=F32) + b2_ref[...]
    o_ref[...] = x + gate_ref[...] * o


def _conformer(x2d, g, shift, scale, gate, w1, b1, wdw, bdw, lng, lnb, w2, b2):
    n = x2d.shape[0]
    tm = CONF_TM
    hb = tm // CONF_HALO
    nh = n // CONF_HALO
    fwd, inv, gw = _conv_dft_tables()
    gspec = _filter_spectrum(gw, wdw)
    vec = lambda w=D_MODEL: _const_spec((1, w))
    return pl.pallas_call(
        functools.partial(_conf_kernel, seq_len=n),
        grid=(n // tm,),
        in_specs=[pl.BlockSpec((CONF_HALO, D_MODEL), lambda i: (jnp.maximum(i * hb - 1, 0), 0)),
                  pl.BlockSpec((tm, D_MODEL), lambda i: (i, 0)),
                  pl.BlockSpec((CONF_HALO, D_MODEL), lambda i: (jnp.minimum((i + 1) * hb, nh - 1), 0)),
                  vec(), vec(), vec(), vec(),
                  _const_spec((D_MODEL, 2 * D_MODEL)), vec(2 * D_MODEL),
                  _const_spec((2 * CONF_NF, CONF_N)), _const_spec((tm, 2 * CONF_NF)),
                  _const_spec((2 * CONF_NF, D_MODEL)), vec(), vec(), vec(),
                  _const_spec((D_MODEL, D_MODEL)), vec()],
        out_specs=pl.BlockSpec((tm, D_MODEL), lambda i: (i, 0)),
        out_shape=jax.ShapeDtypeStruct((n, D_MODEL), F32),
        compiler_params=_cparams(1),
        name="conformer",
    )(x2d, x2d, x2d, g, shift, scale, gate, w1, b1, fwd, inv, gspec, bdw, lng, lnb, w2, b2)


def kernel(x, c, ctx, c_ctx, w_ada, b_ada, g_mix, g_ffn, w_in_hyb, q_gain, k_gain, w_out_hyb,
           w_pw1, b_pw1, w_cdw, b_cdw, ln_g, ln_b, w_pw2, b_pw2, w_up, w_fdw, b_fdw, w_down):
    batch, seq, d = x.shape
    assert batch == 1 and d == D_MODEL
    x2d = x.reshape(seq, d)
    ctx2d = ctx.reshape(-1, d)
    row = lambda v: v.reshape(1, -1)

    cond = jnp.zeros((SUBLANES, d), F32).at[0].set(c[0]).at[1].set(c_ctx)
    mods = _ada(cond, w_ada, b_ada)
    mod = lambda layer, who, j: mods[layer, who:who + 1, j * d:(j + 1) * d]

    w_in = w_in_hyb[0]
    qg = row(jnp.tile(q_gain[0], LANES // HEAD_DIM))
    kg = row(jnp.tile(k_gain[0], LANES // HEAD_DIM))
    bd = _headnorm_matrix()
    dft = _channel_dft_matrix()
    shift2 = mods[0, 0:2, 0:d].reshape(2, 1, d)
    scale2 = mods[0, 0:2, d:2 * d].reshape(2, 1, d)
    qt, k_h, vt, ab = _inproj(x2d, ctx2d, row(g_mix[0]), shift2, scale2, w_in, qg, kg,
                              _rope_tables(seq), bd, dft)
    score_bound = HEAD_DIM * Q_SCALE * jnp.max(jnp.abs(q_gain[0])) * jnp.max(jnp.abs(k_gain[0]))
    att = _attention(qt, k_h, vt, score_bound)
    fm = _fourier(ab)
    ffn_w = (w_up, w_fdw, b_fdw.reshape(b_fdw.shape[0], 1, -1), w_down)
    x2 = _mix_ffn(x2d, att, fm, w_out_hyb[0].astype(BF16), mod(0, 0, 2),
                  row(g_ffn[0]), mod(0, 0, 3), mod(0, 0, 4), mod(0, 0, 5), 0, *ffn_w)

    x3 = _conformer(x2, row(g_mix[1]), mod(1, 0, 0), mod(1, 0, 1), mod(1, 0, 2),
                    w_pw1[0], row(b_pw1[0]), w_cdw[0], row(b_cdw[0]),
                    row(ln_g[0]), row(ln_b[0]), w_pw2[0], row(b_pw2[0]))
    x4 = _ffn(x3, row(g_ffn[1]), mod(1, 0, 3), mod(1, 0, 4), mod(1, 0, 5), 1, *ffn_w)
    return x4.reshape(batch, seq, d)
```

```python
import functools
import math

import numpy as np
import jax
import jax.numpy as jnp
from jax import lax
from jax.experimental import pallas as pl
from jax.experimental.pallas import tpu as pltpu

F32 = jnp.float32
BF16 = jnp.bfloat16

D_MODEL = 1024
GRID_W = 64
HEAD_DIM = 64
N_Q_HEADS = 8
N_KV_HEADS = 2
GQA_GROUP = N_Q_HEADS // N_KV_HEADS
Q_W = N_Q_HEADS * HEAD_DIM
KV_W = N_KV_HEADS * HEAD_DIM
F_GROUPS = 8
F_GROUP_DIM = 64
F_W = F_GROUPS * F_GROUP_DIM
HYB_IN = Q_W + 2 * KV_W + F_W
ROPE_HALF = HEAD_DIM // 2
ROPE_THETA = 10000.0
CONV_WIDTH = 31
FFN_DIM = 2816
NORM_EPS = 1e-6
LN_EPS = 1e-5

LANES = 128
SUBLANES = 8
FFT_N = 128
V_ROWS = 80
VMEM_LIMIT = 60 * 1024 * 1024

Q_SCALE = HEAD_DIM ** -0.5 * math.log2(math.e)


def _cparams(n_axes=1):
    return pltpu.CompilerParams(dimension_semantics=("arbitrary",) * n_axes,
                                vmem_limit_bytes=VMEM_LIMIT)


def _const_spec(shape):
    zeros = (0,) * len(shape)
    return pl.BlockSpec(shape, lambda *_: zeros, pipeline_mode=pl.Buffered(1))


def _rms_mod(x, g, shift, scale):
    ms = jnp.mean(x * x, axis=-1, keepdims=True)
    return (x * lax.rsqrt(ms + NORM_EPS) * g) * (1.0 + scale) + shift


def _silu(x):
    return x * (1.0 / (1.0 + jnp.exp(-x)))


def _ada_kernel(cond_ref, w_ref, b_ref, o_ref):
    def split(v):
        hi = v.astype(BF16)
        return hi, (v - hi.astype(F32)).astype(BF16)

    s_hi, s_lo = split(_silu(cond_ref[...]))
    w_hi, w_lo = split(w_ref[0])
    dot = functools.partial(jnp.dot, preferred_element_type=F32)
    o_ref[0] = dot(s_hi, w_hi) + dot(s_lo, w_hi) + dot(s_hi, w_lo) + b_ref[0]


def _ada(cond, w_ada, b_ada):
    depth, d, n = w_ada.shape
    tn = 1536
    return pl.pallas_call(
        _ada_kernel,
        grid=(depth, n // tn),
        in_specs=[pl.BlockSpec((SUBLANES, d), lambda i, j: (0, 0)),
                  pl.BlockSpec((1, d, tn), lambda i, j: (i, 0, j)),
                  pl.BlockSpec((1, 1, tn), lambda i, j: (i, 0, j))],
        out_specs=pl.BlockSpec((1, SUBLANES, tn), lambda i, j: (i, 0, j)),
        out_shape=jax.ShapeDtypeStruct((depth, SUBLANES, n), F32),
        compiler_params=_cparams(2),
        name="ada",
    )(cond, w_ada, b_ada.reshape(depth, 1, n))


INPROJ_TM = ATTN_TQ = 256
GRID_ROWS_PER_TILE = INPROJ_TM // GRID_W


def _inproj_kernel(x_ref, ctx_ref, gm_ref, sh_ref, sc_ref, w_ref, qg_ref, kg_ref, rcos_ref, rsin_ref,
                   ccos_ref, csin_ref, bd_ref, dft_ref, qt_ref, k_ref, vt_ref, ab_ref, *, n_lat):
    i = pl.program_id(0)
    is_ctx = i >= n_lat
    xin = jnp.where(is_ctx, ctx_ref[...], x_ref[...])
    h = _rms_mod(xin, gm_ref[...], sh_ref[...], sc_ref[...])
    u = jnp.dot(h, w_ref[...], preferred_element_type=F32)

    def rope_table(row_ref, col_ref, ctx_value):
        rows = [jnp.tile(row_ref[g * SUBLANES:(g + 1) * SUBLANES, :], (GRID_W // SUBLANES, 1))
                for g in range(GRID_ROWS_PER_TILE)]
        t = jnp.concatenate(rows, axis=0) + jnp.tile(col_ref[...], (GRID_ROWS_PER_TILE, 1))
        return jnp.where(is_ctx, ctx_value, t)

    cos = rope_table(rcos_ref, ccos_ref, 1.0)
    sin = rope_table(rsin_ref, csin_ref, 0.0)
    bd = bd_ref[...]
    lane = lax.broadcasted_iota(jnp.int32, cos.shape, 1)
    first = (lane % ROPE_HALF) < (ROPE_HALF // 2)

    def head_norm_rope(t, gain):
        sq = t * t
        hi = sq.astype(BF16)
        lo = (sq - hi.astype(F32)).astype(BF16)
        ms = jnp.dot(jnp.concatenate([hi, lo], axis=1), bd, preferred_element_type=F32)
        tn = t * lax.rsqrt(ms + NORM_EPS) * gain
        partner = jnp.where(first, pltpu.roll(tn, LANES - ROPE_HALF // 2, 1),
                            pltpu.roll(tn, ROPE_HALF // 2, 1))
        return tn * cos + partner * sin

    kk = head_norm_rope(u[:, Q_W:Q_W + KV_W], kg_ref[...]).astype(BF16)
    vt = u[:, Q_W + KV_W:Q_W + 2 * KV_W].T
    pad_rows = lax.broadcasted_iota(jnp.int32, (V_ROWS - HEAD_DIM, INPROJ_TM), 0)
    ones_rows = jnp.where(pad_rows == 0, 1.0, 0.0).astype(BF16)
    for kv in range(N_KV_HEADS):
        k_ref[kv] = kk[:, kv * HEAD_DIM:(kv + 1) * HEAD_DIM]
        vt_ref[kv, :HEAD_DIM, :] = vt[kv * HEAD_DIM:(kv + 1) * HEAD_DIM, :].astype(BF16)
        vt_ref[kv, HEAD_DIM:, :] = ones_rows

    @pl.when(i < n_lat)
    def _():
        qg = qg_ref[...]
        for j in range(Q_W // LANES):
            t = head_norm_rope(u[:, j * LANES:(j + 1) * LANES], qg) * Q_SCALE
            tt = t.astype(BF16).T
            for hh in range(LANES // HEAD_DIM):
                head = j * (LANES // HEAD_DIM) + hh
                kv, g = head // GQA_GROUP, head % GQA_GROUP
                qt_ref[kv, :, g * INPROJ_TM:(g + 1) * INPROJ_TM] = tt[hh * HEAD_DIM:(hh + 1) * HEAD_DIM, :]
        f = u[:, Q_W + 2 * KV_W:].astype(BF16)
        ab_ref[...] = jnp.dot(f, dft_ref[...], preferred_element_type=F32).astype(BF16)


def _inproj(x2d, ctx2d, gm, shift2, scale2, w_in, qg, kg, rope, bd, dft):
    n = x2d.shape[0]
    tm = INPROJ_TM
    n_lat = n // tm
    assert ctx2d.shape[0] == tm and n % tm == 0
    lk = n + tm
    lat = lambda i: jnp.minimum(i, n_lat - 1)
    vec = lambda w: _const_spec((1, w))
    mod = pl.BlockSpec((None, 1, D_MODEL), lambda i: (i // n_lat, 0, 0))
    rtab = pl.BlockSpec((GRID_ROWS_PER_TILE * SUBLANES, LANES), lambda i: (lat(i), 0))
    return pl.pallas_call(
        functools.partial(_inproj_kernel, n_lat=n_lat),
        grid=(n_lat + 1,),
        in_specs=[pl.BlockSpec((tm, D_MODEL), lambda i: (lat(i), 0)), _const_spec((tm, D_MODEL)),
                  vec(D_MODEL), mod, mod, _const_spec((D_MODEL, HYB_IN)), vec(LANES), vec(LANES),
                  rtab, rtab, _const_spec((GRID_W, LANES)), _const_spec((GRID_W, LANES)),
                  _const_spec((2 * LANES, LANES)), _const_spec((F_W, 2 * F_W))],
        out_specs=[pl.BlockSpec((N_KV_HEADS, None, HEAD_DIM, GQA_GROUP * tm), lambda i: (0, lat(i), 0, 0)),
                   pl.BlockSpec((N_KV_HEADS, tm, HEAD_DIM), lambda i: (0, i, 0)),
                   pl.BlockSpec((N_KV_HEADS, V_ROWS, tm), lambda i: (0, 0, i)),
                   pl.BlockSpec((tm, 2 * F_W), lambda i: (lat(i), 0))],
        out_shape=[jax.ShapeDtypeStruct((N_KV_HEADS, n_lat, HEAD_DIM, GQA_GROUP * tm), BF16),
                   jax.ShapeDtypeStruct((N_KV_HEADS, lk, HEAD_DIM), BF16),
                   jax.ShapeDtypeStruct((N_KV_HEADS, V_ROWS, lk), BF16),
                   jax.ShapeDtypeStruct((n, 2 * F_W), BF16)],
        compiler_params=_cparams(1),
        name="inproj",
    )(x2d, ctx2d, gm, shift2, scale2, w_in, qg, kg, *rope, bd, dft)


def _rope_tables(n):
    inv_freq = 1.0 / (ROPE_THETA ** (np.arange(0, ROPE_HALF, 2, dtype=np.float64) / ROPE_HALF))
    rows = n // GRID_W
    e = np.arange(ROPE_HALF)
    sign = np.where(e < ROPE_HALF // 2, -1.0, 1.0)
    ang_r = np.arange(rows)[:, None] * inv_freq[e % (ROPE_HALF // 2)][None, :]
    ang_c = np.arange(GRID_W)[:, None] * inv_freq[e % (ROPE_HALF // 2)][None, :]

    def two_heads(row_half, col_half):
        t = np.concatenate([row_half, col_half], axis=1)
        return np.concatenate([t, t], axis=1)

    zr, zc = np.zeros_like(ang_r), np.zeros_like(ang_c)
    rep = lambda t: jnp.asarray(np.repeat(t, SUBLANES, axis=0), F32)
    return (rep(two_heads(np.cos(ang_r), zr)), rep(two_heads(np.sin(ang_r) * sign, zr)),
            jnp.asarray(two_heads(zc, np.cos(ang_c)), F32),
            jnp.asarray(two_heads(zc, np.sin(ang_c) * sign), F32))


def _headnorm_matrix():
    i = np.arange(LANES)
    m = (i[:, None] // HEAD_DIM == i[None, :] // HEAD_DIM) / HEAD_DIM
    return jnp.asarray(np.concatenate([m, m], axis=0), BF16)


def _channel_dft_matrix():
    i = np.arange(F_W)
    same = (i[:, None] // F_GROUP_DIM == i[None, :] // F_GROUP_DIM)
    ang = 2.0 * np.pi * ((i[:, None] % F_GROUP_DIM) * (i[None, :] % F_GROUP_DIM) % F_GROUP_DIM) / F_GROUP_DIM
    s = F_GROUP_DIM ** -0.5
    return jnp.asarray(np.concatenate([np.cos(ang) * same * s, np.sin(ang) * same * s], axis=1), BF16)


ATTN_TK = 256
ATTN_UNROLL = 65
ATTN_SLOT = tuple(s % 6 for s in range(ATTN_UNROLL))
ATTN_NSLOT = max(ATTN_SLOT) + 1
ATTN_UNSHIFTED_LIMIT = 80.0
ATTN_M_INIT = -1e30


def _attn_kernel(qt_ref, k_ref, vt_ref, o_ref, *scr, tk, unroll, running_max):
    s_scr = scr[:ATTN_NSLOT]
    p_scr = scr[ATTN_NSLOT:2 * ATTN_NSLOT]
    acc_scr = scr[2 * ATTN_NSLOT]
    if running_max:
        m_scr = scr[2 * ATTN_NSLOT + 1]
        a_scr = scr[2 * ATTN_NSLOT + 2:]
    slot_of = lambda s: ATTN_SLOT[s % unroll]
    nb = k_ref.shape[0] // tk
    qt = qt_ref[...]

    def scores(blk, slot):
        off = pl.multiple_of(blk * tk, tk)
        s_scr[slot][...] = jnp.dot(k_ref[pl.ds(off, tk), :], qt, preferred_element_type=F32)

    def probs(slot):
        s = s_scr[slot][...]
        if running_max:
            m_old = m_scr[...]
            m_new = jnp.maximum(m_old, jnp.max(s, axis=0, keepdims=True))
            a_scr[slot][...] = jnp.exp2(m_old - m_new)
            m_scr[...] = m_new
            s = s - m_new
        p_scr[slot][...] = jnp.exp2(s)

    def accumulate(blk, slot):
        off = pl.multiple_of(blk * tk, tk)
        d = jnp.dot(vt_ref[:, pl.ds(off, tk)].astype(F32), p_scr[slot][...], preferred_element_type=F32)
        if running_max:
            acc_scr[...] = acc_scr[...] * a_scr[slot][...] + d
        else:
            acc_scr[...] += d

    def steps(j, n_probs, n_scores):
        for s in range(unroll):
            accumulate(j * unroll + s, slot_of(s))
            if s < n_probs:
                probs(slot_of(s + 1))
            if s < n_scores:
                scores(j * unroll + s + 2, slot_of(s + 2))

    def body(j, carry):
        steps(j, unroll, unroll)
        return carry

    acc_scr[...] = jnp.zeros_like(acc_scr)
    if running_max:
        m_scr[...] = jnp.full_like(m_scr, ATTN_M_INIT)
    scores(0, slot_of(0))
    scores(1, slot_of(1))
    probs(slot_of(0))
    n_iter = nb // unroll
    lax.fori_loop(0, n_iter - 1, body, 0)
    steps(n_iter - 1, unroll - 1, unroll - 2)
    acc = acc_scr[...]
    o = acc[:HEAD_DIM, :] * (1.0 / acc[HEAD_DIM:HEAD_DIM + 1, :])
    tq = o.shape[1] // GQA_GROUP
    outs = [o[:, g * tq:(g + 1) * tq].T for g in range(GQA_GROUP)]
    o_ref[...] = jnp.concatenate(outs, axis=1).astype(o_ref.dtype)


def _attention_call(qt, k, vt, *, running_max):
    tq, tk, unroll = ATTN_TQ, ATTN_TK, ATTN_UNROLL
    nq = qt.shape[1]
    lk = k.shape[1]
    assert lk % (tk * unroll) == 0 and unroll >= 3
    gw = GQA_GROUP * HEAD_DIM
    nlane = GQA_GROUP * tq
    scratch = ([pltpu.VMEM((tk, nlane), F32)] * ATTN_NSLOT
               + [pltpu.VMEM((tk, nlane), F32)] * ATTN_NSLOT
               + [pltpu.VMEM((V_ROWS, nlane), F32)])
    if running_max:
        scratch += [pltpu.VMEM((1, nlane), F32)] * (1 + ATTN_NSLOT)
    return pl.pallas_call(
        functools.partial(_attn_kernel, tk=tk, unroll=unroll, running_max=running_max),
        grid=(N_KV_HEADS, nq),
        in_specs=[pl.BlockSpec((None, None, HEAD_DIM, nlane), lambda h, i: (h, i, 0, 0)),
                  pl.BlockSpec((None, lk, HEAD_DIM), lambda h, i: (h, 0, 0)),
                  pl.BlockSpec((None, V_ROWS, lk), lambda h, i: (h, 0, 0))],
        out_specs=pl.BlockSpec((tq, gw), lambda h, i: (i, h)),
        out_shape=jax.ShapeDtypeStruct((nq * tq, Q_W), BF16),
        scratch_shapes=scratch,
        compiler_params=_cparams(2),
        name="attention_running_max" if running_max else "attention",
    )(qt, k, vt)


def _attention(qt, k, vt, score_bound):
    return lax.cond(score_bound < ATTN_UNSHIFTED_LIMIT,
                    functools.partial(_attention_call, running_max=False),
                    functools.partial(_attention_call, running_max=True), qt, k, vt)


def _fft1_kernel(ab_ref, ca_ref, cb_ref, tr_ref, ti_ref, *, nblk):
    ca = ca_ref[...]
    cb = cb_ref[...]
    for j in range(nblk):
        a = ab_ref[:, j * 2 * F_W:j * 2 * F_W + F_W]
        b = ab_ref[:, j * 2 * F_W + F_W:(j + 1) * 2 * F_W]
        t = (jnp.dot(ca, a, preferred_element_type=F32)
             + jnp.dot(cb, b, preferred_element_type=F32))
        tr_ref[:, j * F_W:(j + 1) * F_W] = t[:FFT_N].astype(BF16)
        ti_ref[:, j * F_W:(j + 1) * F_W] = t[FFT_N:].astype(BF16)


def _fft2_kernel(tr_ref, ti_ref, m_ref, y_ref, *, nblk):
    for j in range(nblk):
        m = m_ref[j]
        y = (jnp.dot(m[:, :FFT_N], tr_ref[j * FFT_N:(j + 1) * FFT_N, :], preferred_element_type=F32)
             + jnp.dot(m[:, FFT_N:], ti_ref[j * FFT_N:(j + 1) * FFT_N, :], preferred_element_type=F32))
        y_ref[:, j * F_W:(j + 1) * F_W] = y.astype(BF16)


def _fft_tables():
    n = FFT_N
    l = n * n
    s = n ** -0.5
    k = np.arange(n)
    ang1 = 2.0 * np.pi * ((k[:, None] * k[None, :]) % n) / n
    c1, s1 = np.cos(ang1) * s, np.sin(ang1) * s
    ca = np.concatenate([c1, -s1], axis=0)
    cb = np.concatenate([-s1, -c1], axis=0)
    k1 = k[:, None, None]
    k2 = k[None, :, None]
    n2 = k[None, None, :]
    ang2 = 2.0 * np.pi * ((n2 * (k1 + n * k2)) % l) / l
    m = np.concatenate([np.cos(ang2) * s, np.sin(ang2) * s], axis=2)
    return jnp.asarray(ca, BF16), jnp.asarray(cb, BF16), jnp.asarray(m, BF16)


def _fourier(ab):
    l = ab.shape[0]
    assert l == FFT_N * FFT_N
    ca, cb, m = _fft_tables()
    nblk = 32
    ab_v = ab.reshape(FFT_N, FFT_N * 2 * F_W)
    tr, ti = pl.pallas_call(
        functools.partial(_fft1_kernel, nblk=nblk),
        grid=(FFT_N // nblk,),
        in_specs=[pl.BlockSpec((FFT_N, nblk * 2 * F_W), lambda i: (0, i)),
                  _const_spec((2 * FFT_N, FFT_N)), _const_spec((2 * FFT_N, FFT_N))],
        out_specs=[pl.BlockSpec((FFT_N, nblk * F_W), lambda i: (0, i))] * 2,
        out_shape=[jax.ShapeDtypeStruct((FFT_N, FFT_N * F_W), BF16)] * 2,
        compiler_params=_cparams(1),
        name="fft_stage1",
    )(ab_v, ca, cb)
    tr = tr.reshape(l, F_W)
    ti = ti.reshape(l, F_W)
    y = pl.pallas_call(
        functools.partial(_fft2_kernel, nblk=nblk),
        grid=(FFT_N // nblk,),
        in_specs=[pl.BlockSpec((nblk * FFT_N, F_W), lambda i: (i, 0)),
                  pl.BlockSpec((nblk * FFT_N, F_W), lambda i: (i, 0)),
                  pl.BlockSpec((nblk, FFT_N, 2 * FFT_N), lambda i: (i, 0, 0))],
        out_specs=pl.BlockSpec((FFT_N, nblk * F_W), lambda i: (0, i)),
        out_shape=jax.ShapeDtypeStruct((FFT_N, FFT_N * F_W), BF16),
        compiler_params=_cparams(1),
        name="fft_stage2",
    )(tr, ti, m)
    return y.reshape(l, F_W)


FFN_TM = 512
FFN_HALO = SUBLANES
FFN_CHUNKS = 1
BF16_ROWS = 2 * SUBLANES


def _ffn_tile(xp, x, xn, g_ref, sh_ref, sc_ref, gate_ref, wup_ref, wdw_ref, bdw_ref, wdn_ref):
    i = pl.program_id(0)
    last = pl.num_programs(0) - 1
    tm = x.shape[0]
    g, sh, sc = g_ref[...], sh_ref[...], sc_ref[...]
    hp = _rms_mod(xp, g, sh, sc) * jnp.where(i > 0, 1.0, 0.0)
    hn = _rms_mod(xn, g, sh, sc) * jnp.where(i < last, 1.0, 0.0)
    h = jnp.concatenate([hp, _rms_mod(x, g, sh, sc), hn], axis=0)
    rows = tm + 2 * FFN_HALO
    cw = FFN_DIM // FFN_CHUNKS

    def conv(u, c0):
        w = wdw_ref[:, c0:c0 + cw]
        um = pltpu.roll(u, 1, 0)[FFN_HALO:FFN_HALO + tm]
        up = pltpu.roll(u, rows - 1, 0)[FFN_HALO:FFN_HALO + tm]
        return (um * w[0:1] + u[FFN_HALO:FFN_HALO + tm] * w[1:2] + up * w[2:3]
                + bdw_ref[:, c0:c0 + cw])

    acc = None
    for c in range(FFN_CHUNKS):
        ca, cb = c * cw, FFN_DIM + c * cw
        a = conv(jnp.dot(h, wup_ref[:, ca:ca + cw], preferred_element_type=F32), ca)
        b = conv(jnp.dot(h, wup_ref[:, cb:cb + cw], preferred_element_type=F32), cb)
        act = _silu(a) * b
        d = jnp.dot(act, wdn_ref[ca:ca + cw, :], preferred_element_type=F32)
        acc = d if acc is None else acc + d
    return x + gate_ref[...] * acc


def _ffn_kernel(xp_ref, x_ref, xn_ref, *rest):
    *ffn_refs, o_ref = rest
    o_ref[...] = _ffn_tile(xp_ref[...], x_ref[...], xn_ref[...], *ffn_refs)


def _mix_ffn_kernel(xp_ref, x_ref, xn_ref, ap_ref, a_ref, an_ref, yp_ref, y_ref, yn_ref, wo_ref, g1_ref,
                    *rest):
    *ffn_refs, o_ref = rest
    tm = x_ref.shape[0]
    a_all = jnp.concatenate([a_ref[...], ap_ref[...], an_ref[...]], axis=0)
    y_all = jnp.concatenate([y_ref[...], yp_ref[...], yn_ref[...]], axis=0)
    o_all = (jnp.dot(a_all, wo_ref[:Q_W, :], preferred_element_type=F32)
             + jnp.dot(y_all, wo_ref[Q_W:, :], preferred_element_type=F32))
    g1 = g1_ref[...]
    prev_lo = tm + BF16_ROWS - FFN_HALO
    next_lo = tm + BF16_ROWS
    x1 = x_ref[...] + g1 * o_all[:tm]
    x1p = xp_ref[...] + g1 * o_all[prev_lo:prev_lo + FFN_HALO]
    x1n = xn_ref[...] + g1 * o_all[next_lo:next_lo + FFN_HALO]
    o_ref[...] = _ffn_tile(x1p, x1, x1n, *ffn_refs)


def _halo_specs(n, tm, halo, width):
    hb, nh = tm // halo, n // halo
    return [pl.BlockSpec((halo, width), lambda i: (jnp.maximum(i * hb - 1, 0), 0)),
            pl.BlockSpec((tm, width), lambda i: (i, 0)),
            pl.BlockSpec((halo, width), lambda i: (jnp.minimum((i + 1) * hb, nh - 1), 0))]


def _ffn_weight_specs(layer):
    vec = lambda: _const_spec((1, D_MODEL))
    per_layer = lambda r, c: pl.BlockSpec((None, r, c), lambda i: (layer, 0, 0),
                                          pipeline_mode=pl.Buffered(1))
    return [vec(), vec(), vec(), vec(),
            per_layer(D_MODEL, 2 * FFN_DIM), per_layer(3, 2 * FFN_DIM),
            per_layer(1, 2 * FFN_DIM), per_layer(FFN_DIM, D_MODEL)]


def _ffn(x2d, g, shift, scale, gate, layer, w_up, w_dw, b_dw, w_down):
    n = x2d.shape[0]
    tm = FFN_TM
    return pl.pallas_call(
        _ffn_kernel,
        grid=(n // tm,),
        in_specs=_halo_specs(n, tm, FFN_HALO, D_MODEL) + _ffn_weight_specs(layer),
        out_specs=pl.BlockSpec((tm, D_MODEL), lambda i: (i, 0)),
        out_shape=jax.ShapeDtypeStruct((n, D_MODEL), F32),
        compiler_params=_cparams(1),
        name="ffn",
    )(x2d, x2d, x2d, g, shift, scale, gate, w_up, w_dw, b_dw, w_down)


def _mix_ffn(x2d, att, fm, w_out, gate1, g, shift, scale, gate, layer, w_up, w_dw, b_dw, w_down):
    n = x2d.shape[0]
    tm = FFN_TM
    return pl.pallas_call(
        _mix_ffn_kernel,
        grid=(n // tm,),
        in_specs=(_halo_specs(n, tm, FFN_HALO, D_MODEL) + _halo_specs(n, tm, BF16_ROWS, Q_W)
                  + _halo_specs(n, tm, BF16_ROWS, F_W)
                  + [_const_spec((Q_W + F_W, D_MODEL)), _const_spec((1, D_MODEL))]
                  + _ffn_weight_specs(layer)),
        out_specs=pl.BlockSpec((tm, D_MODEL), lambda i: (i, 0)),
        out_shape=jax.ShapeDtypeStruct((n, D_MODEL), F32),
        compiler_params=_cparams(1),
        name="mix_ffn",
    )(x2d, x2d, x2d, att, att, att, fm, fm, fm, w_out, gate1, g, shift, scale, gate,
      w_up, w_dw, b_dw, w_down)


CONF_HALO = 2 * SUBLANES
CONF_TM = 512
CONF_N = CONF_TM + 2 * CONF_HALO
CONF_NF = 288


def _conv_dft_tables():
    n, nf = CONF_N, CONF_N // 2 + 1
    f = np.arange(CONF_NF)[:, None]
    live = f < nf
    t = np.arange(n)[None, :]
    ang = 2.0 * np.pi * ((f * t) % n) / n
    fwd = np.concatenate([np.cos(ang) * live, np.sin(ang) * live], axis=0)
    r = np.arange(CONF_HALO, CONF_HALO + CONF_TM)[:, None]
    fi = f.T
    weight = np.where((fi == 0) | (fi == n // 2), 1.0, 2.0) * live.T / n
    angi = 2.0 * np.pi * ((r * fi) % n) / n
    inv = np.concatenate([weight * np.cos(angi), -weight * np.sin(angi)], axis=1)
    k = np.arange(CONV_WIDTH)[None, :] - (CONV_WIDTH - 1) // 2
    angw = 2.0 * np.pi * ((f * k) % n) / n
    gw = np.concatenate([np.cos(angw) * live, np.sin(angw) * live], axis=0)
    return jnp.asarray(fwd, F32), jnp.asarray(inv, F32), jnp.asarray(gw, F32)


def _filter_spectrum_kernel(gw_ref, w_ref, o_ref):
    o_ref[...] = jnp.dot(gw_ref[...], w_ref[...], preferred_element_type=F32,
                         precision=lax.Precision.HIGHEST)


def _filter_spectrum(gw, wdw):
    return pl.pallas_call(
        _filter_spectrum_kernel,
        out_shape=jax.ShapeDtypeStruct((2 * CONF_NF, D_MODEL), F32),
        name="conv_filter_spectrum",
    )(gw, wdw)


def _conf_kernel(xp_ref, x_ref, xn_ref, g_ref, sh_ref, sc_ref, gate_ref, w1_ref, b1_ref, fwd_ref, inv_ref,
                 gs_ref, bdw_ref, lng_ref, lnb_ref, w2_ref, b2_ref, o_ref, *, seq_len):
    i = pl.program_id(0)
    tm = x_ref.shape[0]
    g, sh, sc = g_ref[...], sh_ref[...], sc_ref[...]
    x = x_ref[...]
    h = jnp.concatenate([_rms_mod(xp_ref[...], g, sh, sc), _rms_mod(x, g, sh, sc),
                         _rms_mod(xn_ref[...], g, sh, sc)], axis=0)
    u = jnp.dot(h, w1_ref[...], preferred_element_type=F32) + b1_ref[...]
    glu = u[:, :D_MODEL] * (1.0 / (1.0 + jnp.exp(-u[:, D_MODEL:])))
    pos = i * tm - CONF_HALO + lax.broadcasted_iota(jnp.int32, (CONF_N, 1), 0)
    glu = jnp.where((pos >= 0) & (pos < seq_len), glu, 0.0)
    spec = jnp.dot(fwd_ref[...], glu, preferred_element_type=F32)
    uc, us = spec[:CONF_NF], spec[CONF_NF:]
    gc, gs = gs_ref[:CONF_NF, :], gs_ref[CONF_NF:, :]
    y = jnp.concatenate([uc * gc + us * gs, uc * gs - us * gc], axis=0)
    acc = jnp.dot(inv_ref[...], y, preferred_element_type=F32) + bdw_ref[...]
    mu = jnp.mean(acc, axis=-1, keepdims=True)
    xc = acc - mu
    var = jnp.mean(xc * xc, axis=-1, keepdims=True)
    yn = xc * lax.rsqrt(var + LN_EPS) * lng_ref[...] + lnb_ref[...]
    o = jnp.dot(_silu(yn), w2_ref[...], preferred_element_type=F32) + b2_ref[...]
    o_ref[...] = x + gate_ref[...] * o


def _conformer(x2d, g, shift, scale, gate, w1, b1, wdw, bdw, lng, lnb, w2, b2):
    n = x2d.shape[0]
    tm = CONF_TM
    hb = tm // CONF_HALO
    nh = n // CONF_HALO
    fwd, inv, gw = _conv_dft_tables()
    gspec = _filter_spectrum(gw, wdw)
    vec = lambda w=D_MODEL: _const_spec((1, w))
    return pl.pallas_call(
        functools.partial(_conf_kernel, seq_len=n),
        grid=(n // tm,),
        in_specs=[pl.BlockSpec((CONF_HALO, D_MODEL), lambda i: (jnp.maximum(i * hb - 1, 0), 0)),
                  pl.BlockSpec((tm, D_MODEL), lambda i: (i, 0)),
                  pl.BlockSpec((CONF_HALO, D_MODEL), lambda i: (jnp.minimum((i + 1) * hb, nh - 1), 0)),
                  vec(), vec(), vec(), vec(),
                  _const_spec((D_MODEL, 2 * D_MODEL)), vec(2 * D_MODEL),
                  _const_spec((2 * CONF_NF, CONF_N)), _const_spec((tm, 2 * CONF_NF)),
                  _const_spec((2 * CONF_NF, D_MODEL)), vec(), vec(), vec(),
                  _const_spec((D_MODEL, D_MODEL)), vec()],
        out_specs=pl.BlockSpec((tm, D_MODEL), lambda i: (i, 0)),
        out_shape=jax.ShapeDtypeStruct((n, D_MODEL), F32),
        compiler_params=_cparams(1),
        name="conformer",
    )(x2d, x2d, x2d, g, shift, scale, gate, w1, b1, fwd, inv, gspec, bdw, lng, lnb, w2, b2)


def kernel(x, c, ctx, c_ctx, w_ada, b_ada, g_mix, g_ffn, w_in_hyb, q_gain, k_gain, w_out_hyb,
           w_pw1, b_pw1, w_cdw, b_cdw, ln_g, ln_b, w_pw2, b_pw2, w_up, w_fdw, b_fdw, w_down):
    batch, seq, d = x.shape
    assert batch == 1 and d == D_MODEL
    x2d = x.reshape(seq, d)
    ctx2d = ctx.reshape(-1, d)
    row = lambda v: v.reshape(1, -1)

    cond = jnp.zeros((SUBLANES, d), F32).at[0].set(c[0]).at[1].set(c_ctx)
    mods = _ada(cond, w_ada, b_ada)
    mod = lambda layer, who, j: mods[layer, who:who + 1, j * d:(j + 1) * d]

    w_in = w_in_hyb[0]
    qg = row(jnp.tile(q_gain[0], LANES // HEAD_DIM))
    kg = row(jnp.tile(k_gain[0], LANES // HEAD_DIM))
    bd = _headnorm_matrix()
    dft = _channel_dft_matrix()
    shift2 = mods[0, 0:2, 0:d].reshape(2, 1, d)
    scale2 = mods[0, 0:2, d:2 * d].reshape(2, 1, d)
    qt, k_h, vt, ab = _inproj(x2d, ctx2d, row(g_mix[0]), shift2, scale2, w_in, qg, kg,
                              _rope_tables(seq), bd, dft)
    score_bound = HEAD_DIM * Q_SCALE * jnp.max(jnp.abs(q_gain[0])) * jnp.max(jnp.abs(k_gain[0]))
    att = _attention(qt, k_h, vt, score_bound)
    fm = _fourier(ab)
    ffn_w = (w_up, w_fdw, b_fdw.reshape(b_fdw.shape[0], 1, -1), w_down)
    x2 = _mix_ffn(x2d, att, fm, w_out_hyb[0].astype(BF16), mod(0, 0, 2),
                  row(g_ffn[0]), mod(0, 0, 3), mod(0, 0, 4), mod(0, 0, 5), 0, *ffn_w)

    x3 = _conformer(x2, row(g_mix[1]), mod(1, 0, 0), mod(1, 0, 1), mod(1, 0, 2),
                    w_pw1[0], row(b_pw1[0]), w_cdw[0], row(b_cdw[0]),
                    row(ln_g[0]), row(ln_b[0]), w_pw2[0], row(b_pw2[0]))
    x4 = _ffn(x3, row(g_ffn[1]), mod(1, 0, 3), mod(1, 0, 4), mod(1, 0, 5), 1, *ffn_w)
    return x4.reshape(batch, seq, d)
```
